```python
import math
import jax
import jax.numpy as jnp
from jax import lax
import numpy as np

D_MODEL = 1024
BATCH = 16
SEQ = 2048
DEPTH = 1

PLE_DIM = 256
NSA_HEADS = 8
NSA_KV_GROUPS = 2
NSA_REP = NSA_HEADS // NSA_KV_GROUPS
NSA_HEAD_DIM = 64
NSA_WIDTH = NSA_HEADS * NSA_HEAD_DIM
CMP_BLOCK = 32
CMP_STRIDE = 16
SEL_BLOCK = 64
N_SEL = 16
WINDOW = 512
SEL_Q_CHUNK = 16
MLA_HEADS = 8
Q_LORA = 256
KV_LORA = 128
QK_NOPE = 64
QK_ROPE = 32
V_DIM = 64
MLA_WIDTH = MLA_HEADS * V_DIM
ROPE_THETA = 10000.0
MAX_POS_OFFSET = 4096
D_FF = 2816
CONV_WIDTH = 3
Q_BLOCK = 128
EPS = 1e-6
NEG = -1e30
BIG = 1e9
IN_SIZES = (NSA_WIDTH, 6 * NSA_KV_GROUPS * NSA_HEAD_DIM, 3 * NSA_HEADS, Q_LORA, KV_LORA, QK_ROPE, 2 * D_MODEL)
N_IN = NSA_WIDTH + 6 * NSA_KV_GROUPS * NSA_HEAD_DIM + 3 * NSA_HEADS + Q_LORA + KV_LORA + QK_ROPE + 2 * D_MODEL

kernel_name = 'hybrid_nsa_mla_gated_convffn_ple'


def _rmsnorm(x, g):
    xf = x.astype(jnp.float32)
    y = xf * lax.rsqrt(jnp.mean(xf * xf, axis=-1, keepdims=True) + EPS)
    return (y * g.astype(jnp.float32)).astype(x.dtype)


def _masked_softmax(s, mask):
    p = jax.nn.softmax(jnp.where(mask, s, NEG), axis=-1)
    return p * mask.astype(p.dtype)


def _alibi_slopes(n):
    return jnp.asarray(np.array([2.0 ** (-8.0 * (i + 1) / n) for i in range(n)], dtype=np.float32))


def _rope(x, cos, sin):
    half = x.shape[-1] // 2
    x1, x2 = x[..., :half], x[..., half:]
    return jnp.concatenate([x1 * cos - x2 * sin, x2 * cos + x1 * sin], axis=-1)


def _causal_dwconv(u, w, b):
    c = u.shape[-1]
    y = lax.conv_general_dilated(u, w[:, None, :].astype(u.dtype), window_strides=(1,), padding=[(CONV_WIDTH - 1, 0)], dimension_numbers=('NWC', 'WIO', 'NWC'), feature_group_count=c)
    return y + b


def _nsa_mixer(q, kc, vc, ks, vs, kw, vw, gates, pos_k, pos_v, ck_w1, ck_b1, ck_w2, cv_w1, cv_b1, cv_w2):
    B, T = q.shape[0], q.shape[1]
    G, R, dh = NSA_KV_GROUPS, NSA_REP, NSA_HEAD_DIM
    scale = dh ** -0.5
    slopes = _alibi_slopes(NSA_HEADS).reshape(G, R)
    t_idx = jnp.arange(T)

    n_cmp = (T - CMP_BLOCK) // CMP_STRIDE + 1
    blk = jnp.arange(n_cmp)[:, None] * CMP_STRIDE + jnp.arange(CMP_BLOCK)[None, :]

    def compress(k, pos, w1, b1, w2):
        kb = k[:, blk] + pos[None, None, :, None, :]
        kb = kb.transpose(0, 1, 3, 2, 4).reshape(B, n_cmp, G, CMP_BLOCK * dh)
        return jax.nn.gelu(kb @ w1 + b1) @ w2

    k_cmp = compress(kc, pos_k, ck_w1, ck_b1, ck_w2)
    v_cmp = compress(vc, pos_v, cv_w1, cv_b1, cv_w2)
    blk_end = jnp.arange(n_cmp) * CMP_STRIDE + CMP_BLOCK - 1
    dist_c = (t_idx[:, None] - blk_end[None, :]).astype(jnp.float32)
    s_c = jnp.einsum('btgrd,bjgd->bgrtj', q, k_cmp, preferred_element_type=jnp.float32) * scale
    s_c = s_c - slopes[:, :, None, None] * dist_c
    p_cmp = _masked_softmax(s_c, dist_c >= 0)
    o_cmp = jnp.einsum('bgrtj,bjgd->btgrd', p_cmp.astype(v_cmp.dtype), v_cmp)

    n_sb = T // SEL_BLOCK
    per = SEL_BLOCK // CMP_STRIDE
    imp = p_cmp.sum(axis=2)
    imp = jnp.pad(imp, ((0, 0), (0, 0), (0, 0), (0, n_sb * per - n_cmp)))
    imp = imp.reshape(B, G, T, n_sb, per).sum(axis=-1)
    cur = t_idx // SEL_BLOCK
    sb = jnp.arange(n_sb)
    forced = (sb[None, :] == 0) | (sb[None, :] == cur[:, None]) | (sb[None, :] == cur[:, None] - 1)
    future = sb[None, :] > cur[:, None]
    score = jnp.where(forced, BIG, jnp.where(future, -BIG, imp))
    k_sel = min(N_SEL, n_sb)
    _, sel_idx = lax.top_k(score, k_sel)

    ksb = ks.reshape(B, n_sb, SEL_BLOCK, G, dh).transpose(0, 3, 1, 2, 4)
    vsb = vs.reshape(B, n_sb, SEL_BLOCK, G, dh).transpose(0, 3, 1, 2, 4)
    C = SEL_Q_CHUNK
    n_ch = T // C
    q_ch = q.reshape(B, n_ch, C, G, R, dh).transpose(1, 0, 2, 3, 4, 5)
    idx_ch = sel_idx.reshape(B, G, n_ch, C, k_sel).transpose(2, 0, 1, 3, 4)
    t_ch = t_idx.reshape(n_ch, C)
    gather = jax.vmap(jax.vmap(lambda kb, ix: kb[ix]))

    def sel_step(args):
        qc, ic, tc = args
        kg = gather(ksb, ic)
        vg = gather(vsb, ic)
        s = jnp.einsum('bcgrd,bgckld->bgrckl', qc, kg, preferred_element_type=jnp.float32) * scale
        spos = ic[..., None] * SEL_BLOCK + jnp.arange(SEL_BLOCK)
        dist = (tc[None, None, :, None, None] - spos).astype(jnp.float32)
        s = s - slopes[None, :, :, None, None, None] * dist[:, :, None]
        s = s.reshape(B, G, R, C, k_sel * SEL_BLOCK)
        mask = (dist >= 0).reshape(B, G, 1, C, k_sel * SEL_BLOCK)
        pr = _masked_softmax(s, mask).reshape(B, G, R, C, k_sel, SEL_BLOCK)
        return jnp.einsum('bgrckl,bgckld->bcgrd', pr.astype(vg.dtype), vg)

    o_sel = lax.map(sel_step, (q_ch, idx_ch, t_ch))
    o_sel = o_sel.transpose(1, 0, 2, 3, 4, 5).reshape(B, T, G, R, dh)

    n_qb = T // Q_BLOCK
    span = WINDOW + Q_BLOCK
    kw_p = jnp.pad(kw, ((0, 0), (WINDOW, 0), (0, 0), (0, 0)))
    vw_p = jnp.pad(vw, ((0, 0), (WINDOW, 0), (0, 0), (0, 0)))
    q_blk = q.reshape(B, n_qb, Q_BLOCK, G, R, dh).transpose(1, 0, 2, 3, 4, 5)

    def win_step(args):
        qb, i = args
        start = i * Q_BLOCK
        kblk = lax.dynamic_slice_in_dim(kw_p, start, span, axis=1)
        vblk = lax.dynamic_slice_in_dim(vw_p, start, span, axis=1)
        tq = start + jnp.arange(Q_BLOCK)
        sk = start - WINDOW + jnp.arange(span)
        dist = (tq[:, None] - sk[None, :]).astype(jnp.float32)
        mask = (dist >= 0) & (dist < WINDOW) & (sk[None, :] >= 0)
        s = jnp.einsum('bqgrd,bsgd->bgrqs', qb, kblk, preferred_element_type=jnp.float32) * scale
        s = s - slopes[:, :, None, None] * dist
        pr = _masked_softmax(s, mask)
        return jnp.einsum('bgrqs,bsgd->bqgrd', pr.astype(vblk.dtype), vblk)

    o_win = lax.map(win_step, (q_blk, jnp.arange(n_qb, dtype=jnp.int32)))
    o_win = o_win.transpose(1, 0, 2, 3, 4, 5).reshape(B, T, G, R, dh)

    g = jax.nn.sigmoid(gates.astype(jnp.float32)).astype(q.dtype).reshape(B, T, 3, G, R, 1)
    o = g[:, :, 0] * o_cmp + g[:, :, 1] * o_sel + g[:, :, 2] * o_win
    return o.reshape(B, T, NSA_WIDTH)


def _mla_mixer(c_q, c_kv, k_rope, cos, sin, g_q, w_uq, g_kv, w_ukv):
    B, T = c_q.shape[0], c_q.shape[1]
    H = MLA_HEADS
    q = (_rmsnorm(c_q, g_q) @ w_uq).reshape(B, T, H, QK_NOPE + QK_ROPE)
    q_nope = q[..., :QK_NOPE]
    q_rope = _rope(q[..., QK_NOPE:], cos[:, :, None, :], sin[:, :, None, :])
    kv = (_rmsnorm(c_kv, g_kv) @ w_ukv).reshape(B, T, H, QK_NOPE + V_DIM)
    k_nope = kv[..., :QK_NOPE]
    v = kv[..., QK_NOPE:]
    k_r = _rope(k_rope, cos, sin)
    scale = (QK_NOPE + QK_ROPE) ** -0.5
    n_qb = T // Q_BLOCK
    qn_b = q_nope.reshape(B, n_qb, Q_BLOCK, H, QK_NOPE).transpose(1, 0, 2, 3, 4)
    qr_b = q_rope.reshape(B, n_qb, Q_BLOCK, H, QK_ROPE).transpose(1, 0, 2, 3, 4)
    s_idx = jnp.arange(T)

    def step(args):
        qn, qr, i = args
        s = (jnp.einsum('bqhd,bshd->bhqs', qn, k_nope, preferred_element_type=jnp.float32) + jnp.einsum('bqhd,bsd->bhqs', qr, k_r, preferred_element_type=jnp.float32)) * scale
        tq = i * Q_BLOCK + jnp.arange(Q_BLOCK)
        pr = _masked_softmax(s, s_idx[None, :] <= tq[:, None])
        return jnp.einsum('bhqs,bshd->bqhd', pr.astype(v.dtype), v)

    o = lax.map(step, (qn_b, qr_b, jnp.arange(n_qb, dtype=jnp.int32)))
    return o.transpose(1, 0, 2, 3, 4).reshape(B, T, MLA_WIDTH)


def setup_inputs(seed: int = 0) -> dict:
    key = jax.random.key(seed)
    ks = jax.random.split(key, 40)
    f32 = jnp.float32

    def nrm(k, shape, fan_in):
        return jax.random.normal(k, shape, f32) * (fan_in ** -0.5)

    def gain(k, n):
        return 1.0 + 0.05 * jax.random.normal(k, (DEPTH, n), f32)

    def small(k, shape, s):
        return s * jax.random.normal(k, shape, f32)

    L = DEPTH
    dh = NSA_HEAD_DIM
    offs = jax.random.randint(ks[2], (BATCH, 1), 0, MAX_POS_OFFSET, dtype=jnp.int32)
    return {
        'x': jax.random.normal(ks[0], (BATCH, SEQ, D_MODEL), f32),
        'p': jax.random.normal(ks[1], (DEPTH, BATCH, SEQ, PLE_DIM), f32),
        'positions': offs + jnp.arange(SEQ, dtype=jnp.int32)[None, :],
        'g_pre_mix': gain(ks[3], D_MODEL),
        'w_in': nrm(ks[4], (L, D_MODEL, N_IN), D_MODEL),
        'nsa_pos_k': small(ks[5], (L, CMP_BLOCK, dh), 0.1),
        'nsa_pos_v': small(ks[6], (L, CMP_BLOCK, dh), 0.1),
        'nsa_ck_w1': nrm(ks[7], (L, CMP_BLOCK * dh, dh), CMP_BLOCK * dh),
        'nsa_ck_b1': small(ks[8], (L, dh), 0.02),
        'nsa_ck_w2': nrm(ks[9], (L, dh, dh), dh),
        'nsa_cv_w1': nrm(ks[10], (L, CMP_BLOCK * dh, dh), CMP_BLOCK * dh),
        'nsa_cv_b1': small(ks[11], (L, dh), 0.02),
        'nsa_cv_w2': nrm(ks[12], (L, dh, dh), dh),
        'mla_g_q': gain(ks[13], Q_LORA),
        'mla_w_uq': nrm(ks[14], (L, Q_LORA, MLA_HEADS * (QK_NOPE + QK_ROPE)), Q_LORA),
        'mla_g_kv': gain(ks[15], KV_LORA),
        'mla_w_ukv': nrm(ks[16], (L, KV_LORA, MLA_HEADS * (QK_NOPE + V_DIM)), KV_LORA),
        'w_br_nsa': nrm(ks[17], (L, NSA_WIDTH, D_MODEL), NSA_WIDTH),
        'w_br_mla': nrm(ks[18], (L, MLA_WIDTH, D_MODEL), MLA_WIDTH),
        'w_o': nrm(ks[19], (L, D_MODEL, D_MODEL), D_MODEL),
        'g_post_mix': gain(ks[20], D_MODEL),
        'g_pre_ffn': gain(ks[21], D_MODEL),
        'w_up': nrm(ks[22], (L, D_MODEL, 2 * D_FF), D_MODEL),
        'w_conv': nrm(ks[23], (L, CONV_WIDTH, 2 * D_FF), CONV_WIDTH),
        'b_conv': small(ks[24], (L, 2 * D_FF), 0.02),
        'w_down': nrm(ks[25], (L, D_FF, D_MODEL), D_FF),
        'g_post_ffn': gain(ks[26], D_MODEL),
        'w_ple': nrm(ks[27], (L, PLE_DIM, D_MODEL), PLE_DIM),
        'w_ple_gate': nrm(ks[28], (L, D_MODEL, D_MODEL), D_MODEL),
        'g_ple': gain(ks[29], D_MODEL),
    }


def reference(x, p, positions, g_pre_mix, w_in, nsa_pos_k, nsa_pos_v, nsa_ck_w1, nsa_ck_b1, nsa_ck_w2, nsa_cv_w1, nsa_cv_b1, nsa_cv_w2, mla_g_q, mla_w_uq, mla_g_kv, mla_w_ukv, w_br_nsa, w_br_mla, w_o, g_post_mix, g_pre_ffn, w_up, w_conv, b_conv, w_down, g_post_ffn, w_ple, w_ple_gate, g_ple):
    B, T, D = x.shape
    G, R, dh = NSA_KV_GROUPS, NSA_REP, NSA_HEAD_DIM
    offsets = [int(v) for v in np.cumsum(np.array(IN_SIZES))[:-1]]
    inv_freq = ROPE_THETA ** (-jnp.arange(0, QK_ROPE, 2, dtype=jnp.float32) / QK_ROPE)
    ang = positions.astype(jnp.float32)[..., None] * inv_freq
    cos = jnp.cos(ang).astype(x.dtype)
    sin = jnp.sin(ang).astype(x.dtype)
    for i in range(DEPTH):
        h = _rmsnorm(x, g_pre_mix[i])
        proj = h @ w_in[i]
        q_n, kv_n, gate_n, c_q, c_kv, k_r, merge = jnp.split(proj, offsets, axis=-1)
        q_n = q_n.reshape(B, T, G, R, dh)
        kv_n = kv_n.reshape(B, T, 6, G, dh)
        o_nsa = _nsa_mixer(q_n, kv_n[:, :, 0], kv_n[:, :, 1], kv_n[:, :, 2], kv_n[:, :, 3], kv_n[:, :, 4], kv_n[:, :, 5], gate_n, nsa_pos_k[i], nsa_pos_v[i], nsa_ck_w1[i], nsa_ck_b1[i], nsa_ck_w2[i], nsa_cv_w1[i], nsa_cv_b1[i], nsa_cv_w2[i])
        o_mla = _mla_mixer(c_q, c_kv, k_r, cos, sin, mla_g_q[i], mla_w_uq[i], mla_g_kv[i], mla_w_ukv[i])
        gm = jax.nn.sigmoid(merge.astype(jnp.float32)).astype(x.dtype).reshape(B, T, 2, D)
        merged = gm[:, :, 0] * (o_nsa @ w_br_nsa[i]) + gm[:, :, 1] * (o_mla @ w_br_mla[i])
        x = x + _rmsnorm(merged @ w_o[i], g_post_mix[i])
        h = _rmsnorm(x, g_pre_ffn[i])
        u = _causal_dwconv(h @ w_up[i], w_conv[i], b_conv[i])
        a, v = u[..., :D_FF], u[..., D_FF:]
        x = x + _rmsnorm((jax.nn.gelu(a) * v) @ w_down[i], g_post_ffn[i])
        e = p[i] @ w_ple[i]
        gate = jax.nn.sigmoid((x @ w_ple_gate[i]).astype(jnp.float32)).astype(x.dtype)
        x = x + _rmsnorm(e * gate, g_ple[i])
    return x
```

```python
import functools
import math

import numpy as np
import jax
import jax.numpy as jnp
from jax import lax
from jax.experimental import pallas as pl
from jax.experimental.pallas import tpu as pltpu

F32 = jnp.float32
BF16 = jnp.bfloat16

D_MODEL = 1024
PLE_DIM = 256
NSA_HEADS = 8
NSA_KV_GROUPS = 2
NSA_REP = NSA_HEADS // NSA_KV_GROUPS
NSA_HEAD_DIM = 64
NSA_WIDTH = NSA_HEADS * NSA_HEAD_DIM
CMP_BLOCK = 32
CMP_STRIDE = 16
SEL_BLOCK = 64
N_SEL = 16
WINDOW = 512
MLA_HEADS = 8
Q_LORA = 256
KV_LORA = 128
QK_NOPE = 64
QK_ROPE = 32
V_DIM = 64
MLA_WIDTH = MLA_HEADS * V_DIM
ROPE_THETA = 10000.0
D_FF = 2816
CONV_WIDTH = 3
EPS = 1e-6
NEG = -1e30
BIG = 1e9
ALIBI_SLOPES = tuple(2.0 ** (-8.0 * (i + 1) / NSA_HEADS) for i in range(NSA_HEADS))

LANES = 128
SUBLANES = 8

FEAT = NSA_HEAD_DIM
SEL_LANE0 = 96
ONES_LANE = NSA_HEAD_DIM

PROJ_ROWS = 512
NSA_TQ = 128
MLA_TQ = 256
MERGE_ROWS = 512
FFN_ROWS = 512
FFN_CHUNK = 256
PLE_ROWS = 512

_C_Q = 0
_C_KC = _C_Q + NSA_HEADS * LANES
_C_VC = _C_KC + LANES
_C_KS = _C_VC + LANES
_C_VS = _C_KS + NSA_KV_GROUPS * LANES
_C_KW = _C_VS + NSA_KV_GROUPS * LANES
_C_VW = _C_KW + NSA_KV_GROUPS * LANES
_C_SMALL = _C_VW + NSA_KV_GROUPS * LANES
_C_CQ = _C_SMALL + LANES
_C_CKV = _C_CQ + Q_LORA
_C_MERGE = _C_CKV + KV_LORA
_C_END = _C_MERGE + 2 * D_MODEL
_SMALL_KR = 32


def _vmem_params(semantics, mib):
    return pltpu.CompilerParams(dimension_semantics=semantics, vmem_limit_bytes=mib * 1024 * 1024)


def _const_spec(shape):
    nd = len(shape)
    return pl.BlockSpec(shape, lambda *_: (0,) * nd, pipeline_mode=pl.Buffered(1))


def _rms(x, g):
    return x * lax.rsqrt(jnp.mean(x * x, axis=-1, keepdims=True) + EPS) * g


def _gelu_tanh(x):
    return 0.5 * x * (1.0 + jnp.tanh(math.sqrt(2.0 / math.pi) * (x + 0.044715 * (x * x * x))))


def _dot(a, b):
    return jnp.dot(a, b, preferred_element_type=F32)


def _dot_nt(a, b):
    return lax.dot_general(a, b, (((1,), (1,)), ((), ())), preferred_element_type=F32)


def _dot_tn(a, b):
    return lax.dot_general(a, b, (((0,), (0,)), ((), ())), preferred_element_type=F32)


def _proj_kernel(x_ref, g_ref, w_ref, wuq_ref, gq_ref, wk_ref, wv_ref, gkv_ref, place_ref, qtab_ref, ktab_ref,
                 rope_ref, q_out, kc_out, vc_out, ks_out, vs_out, kw_out, vw_out, gs_out, qm_out, km_out,
                 vm_out, gm_out):
    rows = x_ref.shape[0]
    h = _rms(x_ref[...], g_ref[...]).astype(BF16)

    def proj(a, b):
        return _dot(h, w_ref[:, a:b])

    lane = lax.broadcasted_iota(jnp.int32, (rows, LANES), 1)
    ones_col = (lane == ONES_LANE).astype(F32)

    qtab = qtab_ref[...]
    for hd in range(NSA_HEADS):
        a = _C_Q + hd * LANES
        q_out[:, hd * LANES:(hd + 1) * LANES] = (proj(a, a + LANES) + ALIBI_SLOPES[hd] * qtab).astype(BF16)

    kc_out[...] = proj(_C_KC, _C_KC + LANES).astype(BF16)
    vc_out[...] = proj(_C_VC, _C_VC + LANES).astype(BF16)

    ktab_sel = ktab_ref[...]
    ktab_win = jnp.where(lane < SEL_LANE0, ktab_sel, 0.0)
    for g in range(NSA_KV_GROUPS):
        sl = slice(g * LANES, (g + 1) * LANES)
        ks_out[:, sl] = (proj(_C_KS + g * LANES, _C_KS + (g + 1) * LANES) + ktab_sel).astype(BF16)
        vs_out[:, sl] = (proj(_C_VS + g * LANES, _C_VS + (g + 1) * LANES) + ones_col).astype(BF16)
        kw_out[:, sl] = (proj(_C_KW + g * LANES, _C_KW + (g + 1) * LANES) + ktab_win).astype(BF16)
        vw_out[:, sl] = (proj(_C_VW + g * LANES, _C_VW + (g + 1) * LANES) + ones_col).astype(BF16)

    small = proj(_C_SMALL, _C_SMALL + LANES)
    gs_out[...] = jax.nn.sigmoid(small).astype(BF16)

    mq = rope_ref[:, 0:LANES]
    rk = rope_ref[:, LANES:2 * LANES]
    t = small * rk
    k_rope = (t + pltpu.roll(t, LANES - QK_ROPE, 1)).astype(BF16)

    cqn = _rms(proj(_C_CQ, _C_CQ + Q_LORA), gq_ref[...]).astype(BF16)
    for hd in range(MLA_HEADS):
        sl = slice(hd * LANES, (hd + 1) * LANES)
        qm_out[:, sl] = (_dot(cqn, wuq_ref[:, sl]) * mq).astype(BF16)

    ckvn = _rms(proj(_C_CKV, _C_CKV + KV_LORA), gkv_ref[...]).astype(BF16)
    for hd in range(MLA_HEADS):
        sl = slice(hd * LANES, (hd + 1) * LANES)
        km_out[:, sl] = (_dot(ckvn, wk_ref[:, sl]) + _dot(k_rope, place_ref[:, sl])).astype(BF16)
        vm_out[:, sl] = (_dot(ckvn, wv_ref[:, sl]) + ones_col).astype(BF16)

    half = D_MODEL // 2
    for c in range(2 * D_MODEL // half):
        a = _C_MERGE + c * half
        gm_out[:, c * half:(c + 1) * half] = jax.nn.sigmoid(proj(a, a + half)).astype(BF16)


def _proj_call(x2d, g_pre, w_all, wuq, g_q, wk, wv, g_kv, place, qtab, ktab, ropetab, seq):
    n = x2d.shape[0]
    rows = PROJ_ROWS
    per_seq = seq // rows

    def row_spec(width):
        return pl.BlockSpec((rows, width), lambda i: (i, 0))

    tab_spec = pl.BlockSpec((rows, LANES), lambda i: (i % per_seq, 0))
    widths = (NSA_HEADS * LANES, LANES, LANES, NSA_KV_GROUPS * LANES, NSA_KV_GROUPS * LANES,
              NSA_KV_GROUPS * LANES, NSA_KV_GROUPS * LANES, LANES, MLA_HEADS * LANES, MLA_HEADS * LANES,
              MLA_HEADS * LANES, 2 * D_MODEL)
    return pl.pallas_call(
        _proj_kernel,
        grid=(n // rows,),
        in_specs=[row_spec(D_MODEL), _const_spec(g_pre.shape), _const_spec(w_all.shape), _const_spec(wuq.shape),
                  _const_spec(g_q.shape), _const_spec(wk.shape), _const_spec(wv.shape), _const_spec(g_kv.shape),
                  _const_spec(place.shape), tab_spec, tab_spec, row_spec(2 * LANES)],
        out_specs=[row_spec(w) for w in widths],
        out_shape=[jax.ShapeDtypeStruct((n, w), BF16) for w in widths],
        compiler_params=_vmem_params(("parallel",), 48),
        name="proj",
    )(x2d, g_pre, w_all, wuq, g_q, wk, wv, g_kv, place, qtab, ktab, ropetab)


def _compress_kernel(kx_ref, vx_ref, w1k_ref, posk_ref, b1k_ref, w2k_ref, w1v_ref, posv_ref, b1v_ref, w2v_ref,
                     ktab_ref, kc_out, vc_out):
    n_sb = kx_ref.shape[1]
    half = w1k_ref.shape[0] // 2
    per = SEL_BLOCK // CMP_STRIDE

    def one(x_ref, w1_ref, pos_ref, b1_ref, w2_ref):
        x = x_ref[0].astype(F32)
        tops, bots = [], []
        for c in range(per):
            xc = x[:, c * half:(c + 1) * half]
            tops.append(_dot((xc + pos_ref[:, 0:half]).astype(BF16), w1_ref[0:half, :]))
            bots.append(_dot((xc + pos_ref[:, half:2 * half]).astype(BF16), w1_ref[half:2 * half, :]))
        nxt = pltpu.roll(bots[0], n_sb - 1, 0)
        row = lax.broadcasted_iota(jnp.int32, nxt.shape, 0)
        nxt = jnp.where(row == n_sb - 1, 0.0, nxt)
        pre = jnp.concatenate([tops[c] + (bots[c + 1] if c + 1 < per else nxt) for c in range(per)], axis=0)
        return _dot(_gelu_tanh(pre + b1_ref[...]).astype(BF16), w2_ref[...])

    kc_out[0] = (one(kx_ref, w1k_ref, posk_ref, b1k_ref, w2k_ref) + ktab_ref[...]).astype(BF16)
    vc_out[0] = one(vx_ref, w1v_ref, posv_ref, b1v_ref, w2v_ref).astype(BF16)


def _compress_call(kx, vx, w1k, posk, b1k, w2k, w1v, posv, b1v, w2v, ktab):
    b, n_sb, width = kx.shape
    n_rows = n_sb * (SEL_BLOCK // CMP_STRIDE)
    x_spec = pl.BlockSpec((1, n_sb, width), lambda i: (i, 0, 0))
    o_spec = pl.BlockSpec((1, n_rows, NSA_KV_GROUPS * LANES), lambda i: (i, 0, 0))
    consts = (w1k, posk, b1k, w2k, w1v, posv, b1v, w2v, ktab)
    return pl.pallas_call(
        _compress_kernel,
        grid=(b,),
        in_specs=[x_spec, x_spec] + [_const_spec(c.shape) for c in consts],
        out_specs=[o_spec, o_spec],
        out_shape=[jax.ShapeDtypeStruct((b, n_rows, NSA_KV_GROUPS * LANES), BF16)] * 2,
        compiler_params=_vmem_params(("parallel",), 32),
        name="compress",
    )(kx, vx, *consts)


def _softmax_tile(qs, k, v, m, acc, mask=None):
    s = _dot_nt(qs, k)
    if mask is not None:
        s = jnp.where(mask, s, NEG)
    m_new = jnp.maximum(m, jnp.max(s, axis=1, keepdims=True))
    alpha = jnp.exp(m - m_new)
    p = jnp.exp(s - m_new)
    acc = alpha * acc + _dot(p.astype(BF16), v)
    return m_new, acc


def _normalize(acc):
    return acc * (1.0 / acc[:, ONES_LANE:ONES_LANE + 1])


def _nsa_kernel(q_ref, kcmp_ref, vcmp_ref, ks_ref, vs_ref, kw_ref, vw_ref, gs_ref, o_ref):
    tq = q_ref.shape[1]
    qi = pl.program_id(1)
    t0 = qi * tq
    m_rows = NSA_REP * tq
    n_cmp_rows = kcmp_ref.shape[1]
    n_sb = n_cmp_rows // (SEL_BLOCK // CMP_STRIDE)
    per = SEL_BLOCK // CMP_STRIDE
    win_tiles = WINDOW // tq

    row_q = lax.broadcasted_iota(jnp.int32, (m_rows, tq), 0) % tq
    col_k = lax.broadcasted_iota(jnp.int32, (m_rows, tq), 1)
    causal = col_k <= row_q
    lower = jnp.logical_and(col_k > row_q, qi >= win_tiles)
    lane_q = lax.broadcasted_iota(jnp.int32, (m_rows, LANES), 1)
    gates = gs_ref[0].astype(F32)

    outs = []
    for g in range(NSA_KV_GROUPS):
        gl = slice(g * LANES, (g + 1) * LANES)
        qg = jnp.concatenate(
            [q_ref[0, :, (g * NSA_REP + r) * LANES:(g * NSA_REP + r + 1) * LANES] for r in range(NSA_REP)], axis=0)

        s_t = _dot_nt(kcmp_ref[0, :, gl], qg)
        krow = lax.broadcasted_iota(jnp.int32, s_t.shape, 0)
        blk_end = (per * (krow % n_sb) + krow // n_sb) * CMP_STRIDE + (CMP_BLOCK - 1)
        t_abs = t0 + lax.broadcasted_iota(jnp.int32, s_t.shape, 1) % tq
        cmask = t_abs >= blk_end
        s_t = jnp.where(cmask, s_t, NEG)
        e = jnp.exp(s_t - jnp.max(s_t, axis=0, keepdims=True))
        p_t = jnp.where(cmask, e, 0.0) * (1.0 / jnp.sum(e, axis=0, keepdims=True))
        o_cmp = _dot_tn(p_t.astype(BF16), vcmp_ref[0, :, gl])

        imp = None
        for r in range(NSA_REP):
            for c in range(per):
                piece = p_t[c * n_sb:(c + 1) * n_sb, r * tq:(r + 1) * tq]
                imp = piece if imp is None else imp + piece
        sb = lax.broadcasted_iota(jnp.int32, (n_sb, tq), 0)
        cur = (t0 + lax.broadcasted_iota(jnp.int32, (n_sb, tq), 1)) // SEL_BLOCK
        forced = (sb == 0) | (sb == cur) | (sb == cur - 1)
        score = jnp.where(forced, BIG, jnp.where(sb > cur, -BIG, imp))
        rank = jnp.zeros((n_sb, tq), jnp.int32)
        for j in range(n_sb):
            sj = score[j:j + 1, :]
            before = (sj > score) | ((sj == score) & (sb > j))
            rank = rank + before.astype(jnp.int32)
        sel_neg = jnp.where(rank < min(N_SEL, n_sb), 0.0, NEG)
        sel_q = jnp.concatenate([jnp.zeros((SEL_LANE0, tq), F32), sel_neg], axis=0).T.astype(BF16)
        q_sel = jnp.where(lane_q >= SEL_LANE0, jnp.concatenate([sel_q] * NSA_REP, axis=0), qg)

        def attend(qs, k_ref, v_ref, lo, first_mask):
            m = jnp.full((m_rows, 1), NEG, F32)
            acc = jnp.zeros((m_rows, LANES), F32)
            if first_mask is not None:
                k0 = pl.multiple_of(jnp.maximum(qi - win_tiles, 0) * tq, tq)
                m, acc = _softmax_tile(qs, k_ref[0, pl.ds(k0, tq), gl], v_ref[0, pl.ds(k0, tq), gl], m, acc,
                                       first_mask)

            def body(kt, carry):
                k0 = pl.multiple_of(kt * tq, tq)
                return _softmax_tile(qs, k_ref[0, pl.ds(k0, tq), gl], v_ref[0, pl.ds(k0, tq), gl], *carry)

            m, acc = lax.fori_loop(lo, qi, body, (m, acc))
            kd = pl.multiple_of(t0, tq)
            m, acc = _softmax_tile(qs, k_ref[0, pl.ds(kd, tq), gl], v_ref[0, pl.ds(kd, tq), gl], m, acc, causal)
            return _normalize(acc)

        o_sel = attend(q_sel, ks_ref, vs_ref, 0, None)
        o_win = attend(qg, kw_ref, vw_ref, jnp.maximum(qi - win_tiles + 1, 0), lower)

        for r in range(NSA_REP):
            hd = g * NSA_REP + r
            rs = slice(r * tq, (r + 1) * tq)
            o = (gates[:, hd:hd + 1] * o_cmp[rs]
                 + gates[:, NSA_HEADS + hd:NSA_HEADS + hd + 1] * o_sel[rs]
                 + gates[:, 2 * NSA_HEADS + hd:2 * NSA_HEADS + hd + 1] * o_win[rs])
            outs.append(o[:, 0:NSA_HEAD_DIM])
    o_ref[0] = jnp.concatenate(outs, axis=1).astype(o_ref.dtype)


def _nsa_call(q, kcmp, vcmp, ks, vs, kw, vw, gs):
    b, t, _ = q.shape
    tq = NSA_TQ
    kv_spec = pl.BlockSpec((1, t, NSA_KV_GROUPS * LANES), lambda i, j: (i, 0, 0))
    cmp_spec = pl.BlockSpec((1, kcmp.shape[1], NSA_KV_GROUPS * LANES), lambda i, j: (i, 0, 0))
    return pl.pallas_call(
        _nsa_kernel,
        grid=(b, t // tq),
        in_specs=[pl.BlockSpec((1, tq, NSA_HEADS * LANES), lambda i, j: (i, j, 0)), cmp_spec, cmp_spec,
                  kv_spec, kv_spec, kv_spec, kv_spec, pl.BlockSpec((1, tq, LANES), lambda i, j: (i, j, 0))],
        out_specs=pl.BlockSpec((1, tq, NSA_WIDTH), lambda i, j: (i, j, 0)),
        out_shape=jax.ShapeDtypeStruct((b, t, NSA_WIDTH), BF16),
        compiler_params=_vmem_params(("parallel", "arbitrary"), 48),
        name="nsa",
    )(q, kcmp, vcmp, ks, vs, kw, vw, gs)


def _mla_kernel(q_ref, k_ref, v_ref, o_ref):
    tq = q_ref.shape[1]
    qi = pl.program_id(1)
    row = lax.broadcasted_iota(jnp.int32, (tq, tq), 0)
    col = lax.broadcasted_iota(jnp.int32, (tq, tq), 1)
    causal = col <= row
    kd = pl.multiple_of(qi * tq, tq)
    outs = []
    for hd in range(MLA_HEADS):
        hl = slice(hd * LANES, (hd + 1) * LANES)
        q = q_ref[0, :, hl]

        def body(kt, carry, hl=hl, q=q):
            k0 = pl.multiple_of(kt * tq, tq)
            return _softmax_tile(q, k_ref[0, pl.ds(k0, tq), hl], v_ref[0, pl.ds(k0, tq), hl], *carry)

        init = (jnp.full((tq, 1), NEG, F32), jnp.zeros((tq, LANES), F32))
        m, acc = lax.fori_loop(0, qi, body, init)
        m, acc = _softmax_tile(q, k_ref[0, pl.ds(kd, tq), hl], v_ref[0, pl.ds(kd, tq), hl], m, acc, causal)
        outs.append(_normalize(acc)[:, 0:V_DIM])
    o_ref[0] = jnp.concatenate(outs, axis=1).astype(o_ref.dtype)


def _mla_call(q, k, v):
    b, t, width = q.shape
    tq = MLA_TQ
    kv_spec = pl.BlockSpec((1, t, width), lambda i, j: (i, 0, 0))
    return pl.pallas_call(
        _mla_kernel,
        grid=(b, t // tq),
        in_specs=[pl.BlockSpec((1, tq, width), lambda i, j: (i, j, 0)), kv_spec, kv_spec],
        out_specs=pl.BlockSpec((1, tq, MLA_WIDTH), lambda i, j: (i, j, 0)),
        out_shape=jax.ShapeDtypeStruct((b, t, MLA_WIDTH), BF16),
        compiler_params=_vmem_params(("parallel", "arbitrary"), 48),
        name="mla",
    )(q, k, v)


def _merge_kernel(on_ref, om_ref, gm_ref, x_ref, wbn_ref, wbm_ref, wo_ref, gpost_ref, gpre_ref, x1_out, h_out):
    a = _dot(on_ref[...], wbn_ref[...])
    b = _dot(om_ref[...], wbm_ref[...])
    merged = gm_ref[:, 0:D_MODEL].astype(F32) * a + gm_ref[:, D_MODEL:2 * D_MODEL].astype(F32) * b
    y = _dot(merged.astype(BF16), wo_ref[...])
    x1 = x_ref[...] + _rms(y, gpost_ref[...])
    x1_out[...] = x1
    h_out[...] = _rms(x1, gpre_ref[...]).astype(BF16)


def _merge_call(o_nsa, o_mla, gm, x2d, wbn, wbm, wo, g_post, g_pre_ffn):
    n = x2d.shape[0]
    rows = MERGE_ROWS

    def row_spec(width):
        return pl.BlockSpec((rows, width), lambda i: (i, 0))

    consts = (wbn, wbm, wo, g_post, g_pre_ffn)
    return pl.pallas_call(
        _merge_kernel,
        grid=(n // rows,),
        in_specs=[row_spec(NSA_WIDTH), row_spec(MLA_WIDTH), row_spec(2 * D_MODEL), row_spec(D_MODEL)]
        + [_const_spec(c.shape) for c in consts],
        out_specs=[row_spec(D_MODEL), row_spec(D_MODEL)],
        out_shape=[jax.ShapeDtypeStruct((n, D_MODEL), F32), jax.ShapeDtypeStruct((n, D_MODEL), BF16)],
        compiler_params=_vmem_params(("parallel",), 40),
        name="merge",
    )(o_nsa, o_mla, gm, x2d, *consts)


def _ffn_kernel(h_ref, x1_ref, wup_ref, wconv_ref, bconv_ref, wdown_ref, g_ref, x2_out, halo_ref, sa_ref, sv_ref,
                acc_ref, *, tiles_per_seq):
    rows = h_ref.shape[0]
    pad = SUBLANES
    keep = CONV_WIDTH - 1
    seq_start = pl.program_id(0) % tiles_per_seq == 0
    h = h_ref[...]

    @pl.when(pl.program_id(0) == 0)
    def _():
        halo_ref[...] = jnp.zeros_like(halo_ref)

    def conv_chunk(off, s_ref):
        cols = slice(off, off + FFN_CHUNK)
        u = _dot(h, wup_ref[:, cols])
        prev = jnp.where(seq_start, 0.0, halo_ref[pad - keep:pad, cols])
        s_ref[pad - keep:pad, :] = prev
        s_ref[pad:pad + rows, :] = u
        halo_ref[pad - keep:pad, cols] = s_ref[pad + rows - keep:pad + rows, :]
        y = wconv_ref[CONV_WIDTH - 1:CONV_WIDTH, cols] * u + bconv_ref[:, cols]
        for k in range(CONV_WIDTH - 1):
            shift = CONV_WIDTH - 1 - k
            y = y + wconv_ref[k:k + 1, cols] * s_ref[pad - shift:pad - shift + rows, :]
        return y

    for c in range(D_FF // FFN_CHUNK):
        a = conv_chunk(c * FFN_CHUNK, sa_ref)
        v = conv_chunk(D_FF + c * FFN_CHUNK, sv_ref)
        part = _dot((_gelu_tanh(a) * v).astype(BF16), wdown_ref[c * FFN_CHUNK:(c + 1) * FFN_CHUNK, :])
        if c == 0:
            acc_ref[...] = part
        else:
            acc_ref[...] += part
    x2_out[...] = x1_ref[...] + _rms(acc_ref[...], g_ref[...])


def _ffn_call(h2, x1, wup, wconv, bconv, wdown, g_post_ffn, seq):
    n = x1.shape[0]
    rows = FFN_ROWS

    def row_spec():
        return pl.BlockSpec((rows, D_MODEL), lambda i: (i, 0))

    consts = (wup, wconv, bconv, wdown, g_post_ffn)
    return pl.pallas_call(
        functools.partial(_ffn_kernel, tiles_per_seq=seq // rows),
        grid=(n // rows,),
        in_specs=[row_spec(), row_spec()] + [_const_spec(c.shape) for c in consts],
        out_specs=row_spec(),
        out_shape=jax.ShapeDtypeStruct((n, D_MODEL), F32),
        scratch_shapes=[pltpu.VMEM((SUBLANES, 2 * D_FF), F32),
                        pltpu.VMEM((rows + SUBLANES, FFN_CHUNK), F32),
                        pltpu.VMEM((rows + SUBLANES, FFN_CHUNK), F32),
                        pltpu.VMEM((rows, D_MODEL), F32)],
        compiler_params=_vmem_params(("arbitrary",), 48),
        name="ffn",
    )(h2, x1, *consts)


def _ple_kernel(p_ref, x_ref, wple_ref, wgate_ref, g_ref, out_ref):
    x = x_ref[...]
    e = _dot(p_ref[...].astype(BF16), wple_ref[...])
    gate = jax.nn.sigmoid(_dot(x.astype(BF16), wgate_ref[...]))
    out_ref[...] = x + _rms(e * gate, g_ref[...])


def _ple_call(p2d, x2, wple, wgate, g_ple):
    n = x2.shape[0]
    rows = PLE_ROWS
    consts = (wple, wgate, g_ple)
    return pl.pallas_call(
        _ple_kernel,
        grid=(n // rows,),
        in_specs=[pl.BlockSpec((rows, PLE_DIM), lambda i: (i, 0)), pl.BlockSpec((rows, D_MODEL), lambda i: (i, 0))]
        + [_const_spec(c.shape) for c in consts],
        out_specs=pl.BlockSpec((rows, D_MODEL), lambda i: (i, 0)),
        out_shape=jax.ShapeDtypeStruct((n, D_MODEL), F32),
        compiler_params=_vmem_params(("parallel",), 32),
        name="ple",
    )(p2d, x2, *consts)


def _pad_heads(w, heads, width):
    rows = w.shape[0]
    w = w.reshape(rows, heads, width)
    return jnp.pad(w, ((0, 0), (0, 0), (0, LANES - width))).reshape(rows, heads * LANES)


def _swap_halves(w):
    half = w.shape[-1] // 2
    return jnp.concatenate([w[..., half:], w[..., :half]], axis=-1)


def _layout_w_in(w_in):
    sizes = (NSA_WIDTH, 6 * NSA_KV_GROUPS * NSA_HEAD_DIM, 3 * NSA_HEADS, Q_LORA, KV_LORA, QK_ROPE, 2 * D_MODEL)
    offs = np.cumsum((0,) + sizes)
    wq, wkv, wgate, wcq, wckv, wkr, wmerge = (w_in[:, offs[i]:offs[i + 1]] for i in range(len(sizes)))
    rows = w_in.shape[0]
    wkv = wkv.reshape(rows, 6, NSA_KV_GROUPS * NSA_HEAD_DIM)
    small = jnp.concatenate([wgate, jnp.zeros((rows, _SMALL_KR - wgate.shape[1]), w_in.dtype), wkr, _swap_halves(wkr),
                             jnp.zeros((rows, LANES - _SMALL_KR - 2 * QK_ROPE), w_in.dtype)], axis=1)
    parts = [_pad_heads(wq * (NSA_HEAD_DIM ** -0.5), NSA_HEADS, NSA_HEAD_DIM), wkv[:, 0], wkv[:, 1]]
    parts += [_pad_heads(wkv[:, i], NSA_KV_GROUPS, NSA_HEAD_DIM) for i in range(2, 6)]
    parts += [small, wcq, wckv, wmerge]
    w_all = jnp.concatenate(parts, axis=1).astype(BF16)
    assert w_all.shape[1] == _C_END
    return w_all


def _layout_compress(w1, pos, b1, w2):
    dh, g, half_tok = NSA_HEAD_DIM, NSA_KV_GROUPS, CMP_BLOCK // 2
    eye = jnp.eye(g, dtype=w1.dtype)
    w1 = w1.reshape(CMP_BLOCK, dh, dh)
    w1e = jnp.einsum("lde,gh->lgdhe", w1, eye).reshape(CMP_BLOCK * g * dh, g * dh)
    pose = jnp.broadcast_to(pos[:, None, :], (CMP_BLOCK, g, dh)).reshape(1, CMP_BLOCK * g * dh)
    b1e = jnp.tile(b1, g).reshape(1, g * dh)
    w2e = jnp.einsum("de,gh->gdhe", w2, eye)
    w2e = jnp.pad(w2e, ((0, 0), (0, 0), (0, 0), (0, LANES - dh))).reshape(g * dh, g * LANES)
    del half_tok
    return w1e.astype(BF16), pose.astype(F32), b1e.astype(F32), w2e.astype(BF16)


def _position_tables(seq):
    t = np.arange(seq)
    hi = (t // SEL_BLOCK) * SEL_BLOCK
    lo = t % SEL_BLOCK
    qtab = np.zeros((seq, LANES), np.float32)
    qtab[:, FEAT + 0] = -hi
    qtab[:, FEAT + 1] = -lo
    qtab[:, FEAT + 2] = 1.0
    qtab[:, FEAT + 3] = 1.0
    ktab = np.zeros((seq, LANES), np.float32)
    ktab[:, FEAT + 0] = 1.0
    ktab[:, FEAT + 1] = 1.0
    ktab[:, FEAT + 2] = hi
    ktab[:, FEAT + 3] = lo
    ktab[t, SEL_LANE0 + t // SEL_BLOCK] = 1.0
    per = SEL_BLOCK // CMP_STRIDE
    n_sb = seq // SEL_BLOCK
    r = np.arange(per * n_sb)
    end = (per * (r % n_sb) + r // n_sb) * CMP_STRIDE + CMP_BLOCK - 1
    ctab = np.zeros((per * n_sb, NSA_KV_GROUPS, LANES), np.float32)
    ctab[:, :, FEAT + 0] = 1.0
    ctab[:, :, FEAT + 1] = 1.0
    ctab[:, :, FEAT + 2] = ((end // SEL_BLOCK) * SEL_BLOCK)[:, None]
    ctab[:, :, FEAT + 3] = (end % SEL_BLOCK)[:, None]
    return jnp.asarray(qtab), jnp.asarray(ktab), jnp.asarray(ctab.reshape(per * n_sb, NSA_KV_GROUPS * LANES))


def _rope_table(positions):
    inv_freq = ROPE_THETA ** (-jnp.arange(0, QK_ROPE, 2, dtype=F32) / QK_ROPE)
    ang = positions.astype(F32)[..., None] * inv_freq
    cos, sin = jnp.cos(ang), jnp.sin(ang)
    c2 = jnp.concatenate([cos, cos], axis=-1)
    s2 = jnp.concatenate([-sin, sin], axis=-1)
    scale = (QK_NOPE + QK_ROPE) ** -0.5
    lead = c2.shape[:-1]
    mq = jnp.concatenate([jnp.full(lead + (QK_NOPE,), scale, F32), scale * c2, scale * s2], axis=-1)
    rk = jnp.concatenate([jnp.zeros(lead + (_SMALL_KR,), F32), c2, s2,
                          jnp.zeros(lead + (LANES - _SMALL_KR - 2 * QK_ROPE,), F32)], axis=-1)
    return jnp.concatenate([mq, rk], axis=-1).reshape(-1, 2 * LANES)


def _rope_placement():
    place = np.zeros((LANES, MLA_HEADS, LANES), np.float32)
    i = np.arange(QK_ROPE)
    place[_SMALL_KR + i, :, QK_NOPE + i] = 1.0
    place[_SMALL_KR + i, :, QK_NOPE + QK_ROPE + i] = 1.0
    return jnp.asarray(place.reshape(LANES, MLA_HEADS * LANES), dtype=BF16)


def kernel(x, p, positions, g_pre_mix, w_in, nsa_pos_k, nsa_pos_v, nsa_ck_w1, nsa_ck_b1, nsa_ck_w2, nsa_cv_w1, nsa_cv_b1, nsa_cv_w2, mla_g_q, mla_w_uq, mla_g_kv, mla_w_ukv, w_br_nsa, w_br_mla, w_o, g_post_mix, g_pre_ffn, w_up, w_conv, b_conv, w_down, g_post_ffn, w_ple, w_ple_gate, g_ple):
    b, t, d = x.shape
    depth = w_in.shape[0]
    assert d == D_MODEL and t // SEL_BLOCK == LANES - SEL_LANE0 and t % PROJ_ROWS == 0
    n = b * t
    qtab, ktab, ctab = _position_tables(t)
    ropetab = _rope_table(positions)
    place = _rope_placement()
    xc = x.reshape(n, d)
    for i in range(depth):
        w_all = _layout_w_in(w_in[i])
        wuq = mla_w_uq[i].reshape(Q_LORA, MLA_HEADS, QK_NOPE + QK_ROPE)
        wuq = jnp.concatenate([wuq, _swap_halves(wuq[..., QK_NOPE:])], axis=-1).reshape(Q_LORA, MLA_HEADS * LANES)
        wukv = mla_w_ukv[i].reshape(KV_LORA, MLA_HEADS, QK_NOPE + V_DIM)
        wk = _pad_heads(wukv[..., :QK_NOPE].reshape(KV_LORA, -1), MLA_HEADS, QK_NOPE)
        wv = _pad_heads(wukv[..., QK_NOPE:].reshape(KV_LORA, -1), MLA_HEADS, V_DIM)
        (q_n, kc, vc, ks, vs, kw, vw, gs, q_m, k_m, v_m, gm) = _proj_call(
            xc, g_pre_mix[i][None], w_all, wuq.astype(BF16), mla_g_q[i][None], wk.astype(BF16), wv.astype(BF16),
            mla_g_kv[i][None], place, qtab, ktab, ropetab, t)

        n_sb = t // SEL_BLOCK
        kcmp, vcmp = _compress_call(
            kc.reshape(b, n_sb, SEL_BLOCK * LANES), vc.reshape(b, n_sb, SEL_BLOCK * LANES),
            *_layout_compress(nsa_ck_w1[i], nsa_pos_k[i], nsa_ck_b1[i], nsa_ck_w2[i]),
            *_layout_compress(nsa_cv_w1[i], nsa_pos_v[i], nsa_cv_b1[i], nsa_cv_w2[i]), ctab)

        def seq3(a):
            return a.reshape(b, t, a.shape[-1])

        o_nsa = _nsa_call(seq3(q_n), kcmp, vcmp, seq3(ks), seq3(vs), seq3(kw), seq3(vw), seq3(gs))
        o_mla = _mla_call(seq3(q_m), seq3(k_m), seq3(v_m))

        x1, h2 = _merge_call(o_nsa.reshape(n, NSA_WIDTH), o_mla.reshape(n, MLA_WIDTH), gm, xc,
                             w_br_nsa[i].astype(BF16), w_br_mla[i].astype(BF16), w_o[i].astype(BF16),
                             g_post_mix[i][None], g_pre_ffn[i][None])
        x2 = _ffn_call(h2, x1, w_up[i].astype(BF16), w_conv[i], b_conv[i][None], w_down[i].astype(BF16),
                       g_post_ffn[i][None], t)
        xc = _ple_call(p[i].reshape(n, PLE_DIM), x2, w_ple[i].astype(BF16), w_ple_gate[i].astype(BF16),
                       g_ple[i][None])
    return xc.reshape(b, t, d)
```

```python
import functools
import math

import numpy as np
import jax
import jax.numpy as jnp
from jax import lax
from jax.experimental import pallas as pl
from jax.experimental.pallas import tpu as pltpu

F32 = jnp.float32
BF16 = jnp.bfloat16

D_MODEL = 1024
PLE_DIM = 256
NSA_HEADS = 8
NSA_KV_GROUPS = 2
NSA_REP = NSA_HEADS // NSA_KV_GROUPS
NSA_HEAD_DIM = 64
NSA_WIDTH = NSA_HEADS * NSA_HEAD_DIM
CMP_BLOCK = 32
CMP_STRIDE = 16
SEL_BLOCK = 64
N_SEL = 16
WINDOW = 512
MLA_HEADS = 8
Q_LORA = 256
KV_LORA = 128
QK_NOPE = 64
QK_ROPE = 32
V_DIM = 64
MLA_WIDTH = MLA_HEADS * V_DIM
ROPE_THETA = 10000.0
D_FF = 2816
CONV_WIDTH = 3
EPS = 1e-6
NEG = -1e30
BIG = 1e9
ALIBI_SLOPES = tuple(2.0 ** (-8.0 * (i + 1) / NSA_HEADS) for i in range(NSA_HEADS))

LANES = 128
SUBLANES = 8

FEAT = NSA_HEAD_DIM
SEL_LANE0 = 96
ONES_LANE = NSA_HEAD_DIM

PROJ_ROWS = 512
NSA_TQ = 128
MLA_TQ = 256
MERGE_ROWS = 512
FFN_ROWS = 512
FFN_CHUNK = 256
PLE_ROWS = 512

_C_Q = 0
_C_KC = _C_Q + NSA_HEADS * LANES
_C_VC = _C_KC + LANES
_C_KS = _C_VC + LANES
_C_VS = _C_KS + NSA_KV_GROUPS * LANES
_C_KW = _C_VS + NSA_KV_GROUPS * LANES
_C_VW = _C_KW + NSA_KV_GROUPS * LANES
_C_SMALL = _C_VW + NSA_KV_GROUPS * LANES
_C_CQ = _C_SMALL + LANES
_C_CKV = _C_CQ + Q_LORA
_C_MERGE = _C_CKV + KV_LORA
_C_END = _C_MERGE + 2 * D_MODEL
_SMALL_KR = 32


def _vmem_params(semantics, mib):
    return pltpu.CompilerParams(dimension_semantics=semantics, vmem_limit_bytes=mib * 1024 * 1024)


def _const_spec(shape):
    nd = len(shape)
    return pl.BlockSpec(shape, lambda *_: (0,) * nd, pipeline_mode=pl.Buffered(1))


def _rms(x, g):
    return x * lax.rsqrt(jnp.mean(x * x, axis=-1, keepdims=True) + EPS) * g


def _gelu_tanh(x):
    return 0.5 * x * (1.0 + jnp.tanh(math.sqrt(2.0 / math.pi) * (x + 0.044715 * (x * x * x))))


def _dot(a, b):
    return jnp.dot(a, b, preferred_element_type=F32)


def _dot_nt(a, b):
    return lax.dot_general(a, b, (((1,), (1,)), ((), ())), preferred_element_type=F32)


def _dot_tn(a, b):
    return lax.dot_general(a, b, (((0,), (0,)), ((), ())), preferred_element_type=F32)


def _proj_kernel(x_ref, g_ref, w_ref, wuq_ref, gq_ref, wk_ref, wv_ref, gkv_ref, place_ref, qtab_ref, ktab_ref,
                 rope_ref, q_out, kc_out, vc_out, ks_out, vs_out, kw_out, vw_out, gs_out, qm_out, km_out,
                 vm_out, gm_out):
    rows = x_ref.shape[0]
    h = _rms(x_ref[...], g_ref[...]).astype(BF16)

    def proj(a, b):
        return _dot(h, w_ref[:, a:b])

    lane = lax.broadcasted_iota(jnp.int32, (rows, LANES), 1)
    ones_col = (lane == ONES_LANE).astype(F32)

    qtab = qtab_ref[...]
    for hd in range(NSA_HEADS):
        a = _C_Q + hd * LANES
        q_out[:, hd * LANES:(hd + 1) * LANES] = (proj(a, a + LANES) + ALIBI_SLOPES[hd] * qtab).astype(BF16)

    kc_out[...] = proj(_C_KC, _C_KC + LANES).astype(BF16)
    vc_out[...] = proj(_C_VC, _C_VC + LANES).astype(BF16)

    ktab_sel = ktab_ref[...]
    ktab_win = jnp.where(lane < SEL_LANE0, ktab_sel, 0.0)
    for g in range(NSA_KV_GROUPS):
        sl = slice(g * LANES, (g + 1) * LANES)
        ks_out[:, sl] = (proj(_C_KS + g * LANES, _C_KS + (g + 1) * LANES) + ktab_sel).astype(BF16)
        vs_out[:, sl] = (proj(_C_VS + g * LANES, _C_VS + (g + 1) * LANES) + ones_col).astype(BF16)
        kw_out[:, sl] = (proj(_C_KW + g * LANES, _C_KW + (g + 1) * LANES) + ktab_win).astype(BF16)
        vw_out[:, sl] = (proj(_C_VW + g * LANES, _C_VW + (g + 1) * LANES) + ones_col).astype(BF16)

    small = proj(_C_SMALL, _C_SMALL + LANES)
    gs_out[...] = jax.nn.sigmoid(small).astype(BF16)

    mq = rope_ref[:, 0:LANES]
    rk = rope_ref[:, LANES:2 * LANES]
    t = small * rk
    k_rope = (t + pltpu.roll(t, LANES - QK_ROPE, 1)).astype(BF16)

    cqn = _rms(proj(_C_CQ, _C_CQ + Q_LORA), gq_ref[...]).astype(BF16)
    for hd in range(MLA_HEADS):
        sl = slice(hd * LANES, (hd + 1) * LANES)
        qm_out[0, hd] = (_dot(cqn, wuq_ref[:, sl]) * mq).astype(BF16)

    ckvn = _rms(proj(_C_CKV, _C_CKV + KV_LORA), gkv_ref[...]).astype(BF16)
    for hd in range(MLA_HEADS):
        sl = slice(hd * LANES, (hd + 1) * LANES)
        km_out[0, hd] = (_dot(ckvn, wk_ref[:, sl]) + _dot(k_rope, place_ref[:, sl])).astype(BF16)
        vm_out[0, hd] = (_dot(ckvn, wv_ref[:, sl]) + ones_col).astype(BF16)

    half = D_MODEL // 2
    for c in range(2 * D_MODEL // half):
        a = _C_MERGE + c * half
        gm_out[:, c * half:(c + 1) * half] = jax.nn.sigmoid(proj(a, a + half)).astype(BF16)


def _proj_call(x2d, g_pre, w_all, wuq, g_q, wk, wv, g_kv, place, qtab, ktab, ropetab, seq):
    n = x2d.shape[0]
    rows = PROJ_ROWS
    per_seq = seq // rows

    def row_spec(width):
        return pl.BlockSpec((rows, width), lambda i: (i, 0))

    tab_spec = pl.BlockSpec((rows, LANES), lambda i: (i % per_seq, 0))
    head_spec = pl.BlockSpec((1, MLA_HEADS, rows, LANES), lambda i: (i // per_seq, 0, i % per_seq, 0))
    head_shape = jax.ShapeDtypeStruct((n // seq, MLA_HEADS, seq, LANES), BF16)
    widths = (NSA_HEADS * LANES, LANES, LANES, NSA_KV_GROUPS * LANES, NSA_KV_GROUPS * LANES,
              NSA_KV_GROUPS * LANES, NSA_KV_GROUPS * LANES, LANES)
    return pl.pallas_call(
        _proj_kernel,
        grid=(n // rows,),
        in_specs=[row_spec(D_MODEL), _const_spec(g_pre.shape), _const_spec(w_all.shape), _const_spec(wuq.shape),
                  _const_spec(g_q.shape), _const_spec(wk.shape), _const_spec(wv.shape), _const_spec(g_kv.shape),
                  _const_spec(place.shape), tab_spec, tab_spec, row_spec(2 * LANES)],
        out_specs=[row_spec(w) for w in widths] + [head_spec] * 3 + [row_spec(2 * D_MODEL)],
        out_shape=[jax.ShapeDtypeStruct((n, w), BF16) for w in widths] + [head_shape] * 3
        + [jax.ShapeDtypeStruct((n, 2 * D_MODEL), BF16)],
        compiler_params=_vmem_params(("parallel",), 48),
        name="proj",
    )(x2d, g_pre, w_all, wuq, g_q, wk, wv, g_kv, place, qtab, ktab, ropetab)


def _compress_kernel(kx_ref, vx_ref, w1k_ref, posk_ref, b1k_ref, w2k_ref, w1v_ref, posv_ref, b1v_ref, w2v_ref,
                     ktab_ref, kc_out, vc_out):
    n_sb = kx_ref.shape[1]
    half = w1k_ref.shape[0] // 2
    per = SEL_BLOCK // CMP_STRIDE

    def one(x_ref, w1_ref, pos_ref, b1_ref, w2_ref):
        x = x_ref[0].astype(F32)
        tops, bots = [], []
        for c in range(per):
            xc = x[:, c * half:(c + 1) * half]
            tops.append(_dot((xc + pos_ref[:, 0:half]).astype(BF16), w1_ref[0:half, :]))
            bots.append(_dot((xc + pos_ref[:, half:2 * half]).astype(BF16), w1_ref[half:2 * half, :]))
        nxt = pltpu.roll(bots[0], n_sb - 1, 0)
        row = lax.broadcasted_iota(jnp.int32, nxt.shape, 0)
        nxt = jnp.where(row == n_sb - 1, 0.0, nxt)
        pre = jnp.concatenate([tops[c] + (bots[c + 1] if c + 1 < per else nxt) for c in range(per)], axis=0)
        return _dot(_gelu_tanh(pre + b1_ref[...]).astype(BF16), w2_ref[...])

    kc_out[0] = (one(kx_ref, w1k_ref, posk_ref, b1k_ref, w2k_ref) + ktab_ref[...]).astype(BF16)
    vc_out[0] = one(vx_ref, w1v_ref, posv_ref, b1v_ref, w2v_ref).astype(BF16)


def _compress_call(kx, vx, w1k, posk, b1k, w2k, w1v, posv, b1v, w2v, ktab):
    b, n_sb, width = kx.shape
    n_rows = n_sb * (SEL_BLOCK // CMP_STRIDE)
    x_spec = pl.BlockSpec((1, n_sb, width), lambda i: (i, 0, 0))
    o_spec = pl.BlockSpec((1, n_rows, NSA_KV_GROUPS * LANES), lambda i: (i, 0, 0))
    consts = (w1k, posk, b1k, w2k, w1v, posv, b1v, w2v, ktab)
    return pl.pallas_call(
        _compress_kernel,
        grid=(b,),
        in_specs=[x_spec, x_spec] + [_const_spec(c.shape) for c in consts],
        out_specs=[o_spec, o_spec],
        out_shape=[jax.ShapeDtypeStruct((b, n_rows, NSA_KV_GROUPS * LANES), BF16)] * 2,
        compiler_params=_vmem_params(("parallel",), 32),
        name="compress",
    )(kx, vx, *consts)


def _softmax_tile(qs, k, v, m, acc, mask=None):
    s = _dot_nt(qs, k)
    if mask is not None:
        s = jnp.where(mask, s, NEG)
    m_new = jnp.maximum(m, jnp.max(s, axis=1, keepdims=True))
    alpha = jnp.exp(m - m_new)
    p = jnp.exp(s - m_new)
    acc = alpha * acc + _dot(p.astype(BF16), v)
    return m_new, acc


def _normalize(acc):
    return acc * (1.0 / acc[:, ONES_LANE:ONES_LANE + 1])


def _nsa_kernel(q_ref, kcmp_ref, vcmp_ref, ks_ref, vs_ref, kw_ref, vw_ref, gs_ref, o_ref):
    tq = q_ref.shape[1]
    qi = pl.program_id(1)
    t0 = qi * tq
    m_rows = NSA_REP * tq
    n_cmp_rows = kcmp_ref.shape[1]
    n_sb = n_cmp_rows // (SEL_BLOCK // CMP_STRIDE)
    per = SEL_BLOCK // CMP_STRIDE
    win_tiles = WINDOW // tq

    row_q = lax.broadcasted_iota(jnp.int32, (m_rows, tq), 0) % tq
    col_k = lax.broadcasted_iota(jnp.int32, (m_rows, tq), 1)
    causal = col_k <= row_q
    lower = jnp.logical_and(col_k > row_q, qi >= win_tiles)
    lane_q = lax.broadcasted_iota(jnp.int32, (m_rows, LANES), 1)
    gates = gs_ref[0].astype(F32)

    outs = []
    for g in range(NSA_KV_GROUPS):
        gl = slice(g * LANES, (g + 1) * LANES)
        qg = jnp.concatenate(
            [q_ref[0, :, (g * NSA_REP + r) * LANES:(g * NSA_REP + r + 1) * LANES] for r in range(NSA_REP)], axis=0)

        s_t = _dot_nt(kcmp_ref[0, :, gl], qg)
        krow = lax.broadcasted_iota(jnp.int32, s_t.shape, 0)
        blk_end = (per * (krow % n_sb) + krow // n_sb) * CMP_STRIDE + (CMP_BLOCK - 1)
        t_abs = t0 + lax.broadcasted_iota(jnp.int32, s_t.shape, 1) % tq
        cmask = t_abs >= blk_end
        s_t = jnp.where(cmask, s_t, NEG)
        e = jnp.exp(s_t - jnp.max(s_t, axis=0, keepdims=True))
        p_t = jnp.where(cmask, e, 0.0) * (1.0 / jnp.sum(e, axis=0, keepdims=True))
        o_cmp = _dot_tn(p_t.astype(BF16), vcmp_ref[0, :, gl])

        imp = None
        for r in range(NSA_REP):
            for c in range(per):
                piece = p_t[c * n_sb:(c + 1) * n_sb, r * tq:(r + 1) * tq]
                imp = piece if imp is None else imp + piece
        sb = lax.broadcasted_iota(jnp.int32, (n_sb, tq), 0)
        cur = (t0 + lax.broadcasted_iota(jnp.int32, (n_sb, tq), 1)) // SEL_BLOCK
        forced = (sb == 0) | (sb == cur) | (sb == cur - 1)
        score = jnp.where(forced, BIG, jnp.where(sb > cur, -BIG, imp))
        rank = jnp.zeros((n_sb, tq), jnp.int32)
        for j in range(n_sb):
            sj = score[j:j + 1, :]
            before = (sj > score) | ((sj == score) & (sb > j))
            rank = rank + before.astype(jnp.int32)
        sel_neg = jnp.where(rank < min(N_SEL, n_sb), 0.0, NEG)
        sel_q = jnp.concatenate([jnp.zeros((SEL_LANE0, tq), F32), sel_neg], axis=0).T.astype(BF16)
        q_sel = jnp.where(lane_q >= SEL_LANE0, jnp.concatenate([sel_q] * NSA_REP, axis=0), qg)

        def attend(qs, k_ref, v_ref, lo, first_mask):
            m = jnp.full((m_rows, 1), NEG, F32)
            acc = jnp.zeros((m_rows, LANES), F32)
            if first_mask is not None:
                k0 = pl.multiple_of(jnp.maximum(qi - win_tiles, 0) * tq, tq)
                m, acc = _softmax_tile(qs, k_ref[0, pl.ds(k0, tq), gl], v_ref[0, pl.ds(k0, tq), gl], m, acc,
                                       first_mask)

            def body(kt, carry):
                k0 = pl.multiple_of(kt * tq, tq)
                return _softmax_tile(qs, k_ref[0, pl.ds(k0, tq), gl], v_ref[0, pl.ds(k0, tq), gl], *carry)

            m, acc = lax.fori_loop(lo, qi, body, (m, acc))
            kd = pl.multiple_of(t0, tq)
            m, acc = _softmax_tile(qs, k_ref[0, pl.ds(kd, tq), gl], v_ref[0, pl.ds(kd, tq), gl], m, acc, causal)
            return _normalize(acc)

        o_sel = attend(q_sel, ks_ref, vs_ref, 0, None)
        o_win = attend(qg, kw_ref, vw_ref, jnp.maximum(qi - win_tiles + 1, 0), lower)

        for r in range(NSA_REP):
            hd = g * NSA_REP + r
            rs = slice(r * tq, (r + 1) * tq)
            o = (gates[:, hd:hd + 1] * o_cmp[rs]
                 + gates[:, NSA_HEADS + hd:NSA_HEADS + hd + 1] * o_sel[rs]
                 + gates[:, 2 * NSA_HEADS + hd:2 * NSA_HEADS + hd + 1] * o_win[rs])
            outs.append(o[:, 0:NSA_HEAD_DIM])
    o_ref[0] = jnp.concatenate(outs, axis=1).astype(o_ref.dtype)


def _nsa_call(q, kcmp, vcmp, ks, vs, kw, vw, gs):
    b, t, _ = q.shape
    tq = NSA_TQ
    kv_spec = pl.BlockSpec((1, t, NSA_KV_GROUPS * LANES), lambda i, j: (i, 0, 0))
    cmp_spec = pl.BlockSpec((1, kcmp.shape[1], NSA_KV_GROUPS * LANES), lambda i, j: (i, 0, 0))
    return pl.pallas_call(
        _nsa_kernel,
        grid=(b, t // tq),
        in_specs=[pl.BlockSpec((1, tq, NSA_HEADS * LANES), lambda i, j: (i, j, 0)), cmp_spec, cmp_spec,
                  kv_spec, kv_spec, kv_spec, kv_spec, pl.BlockSpec((1, tq, LANES), lambda i, j: (i, j, 0))],
        out_specs=pl.BlockSpec((1, tq, NSA_WIDTH), lambda i, j: (i, j, 0)),
        out_shape=jax.ShapeDtypeStruct((b, t, NSA_WIDTH), BF16),
        compiler_params=_vmem_params(("parallel", "arbitrary"), 48),
        name="nsa",
    )(q, kcmp, vcmp, ks, vs, kw, vw, gs)


def _dense_attend(q, k_at, v_at, widths, masks, s_ref):
    mx = None
    off = 0
    for j, w in enumerate(widths):
        s = _dot_nt(q, k_at(j))
        if masks[j] is not None:
            s = jnp.where(masks[j], s, NEG)
        s_ref[:, off:off + w] = s
        for c in range(w // LANES):
            piece = s[:, c * LANES:(c + 1) * LANES]
            mx = piece if mx is None else jnp.maximum(mx, piece)
        off += w
    m = jnp.max(mx, axis=1, keepdims=True)
    acc = None
    off = 0
    for j, w in enumerate(widths):
        part = _dot(jnp.exp(s_ref[:, off:off + w] - m).astype(BF16), v_at(j))
        acc = part if acc is None else acc + part
        off += w
    return _normalize(acc)


def _mla_kernel(q_ref, k_ref, v_ref, o_ref, s_ref):
    tq = q_ref.shape[2]
    nq = k_ref.shape[2] // tq
    qi = pl.program_id(1)
    row = lax.broadcasted_iota(jnp.int32, (tq, tq), 0)
    col = lax.broadcasted_iota(jnp.int32, (tq, tq), 1)
    causal = col <= row
    pair = o_ref.shape[1]
    per_pair = MLA_HEADS // pair

    for n in range(nq):

        @pl.when(qi == n)
        def _(n=n):
            def pair_body(pr, carry):
                outs = []
                for e in range(per_pair):
                    hd = per_pair * pr + e
                    o = _dense_attend(
                        q_ref[0, hd],
                        lambda j, hd=hd: k_ref[0, hd, j * tq:(j + 1) * tq, :],
                        lambda j, hd=hd: v_ref[0, hd, j * tq:(j + 1) * tq, :],
                        [tq] * (n + 1), [None] * n + [causal], s_ref.at[e])
                    outs.append(o[:, 0:V_DIM])
                o_ref[0, pr] = jnp.concatenate(outs, axis=1).astype(o_ref.dtype)
                return carry

            lax.fori_loop(0, pair, pair_body, 0)


def _mla_call(q, k, v):
    b, heads, t, width = q.shape
    tq = MLA_TQ
    pairs = heads * V_DIM // LANES
    kv_spec = pl.BlockSpec((1, heads, t, width), lambda i, j: (i, 0, 0, 0))
    return pl.pallas_call(
        _mla_kernel,
        grid=(b, t // tq),
        in_specs=[pl.BlockSpec((1, heads, tq, width), lambda i, j: (i, 0, j, 0)), kv_spec, kv_spec],
        out_specs=pl.BlockSpec((1, pairs, tq, LANES), lambda i, j: (i, 0, j, 0)),
        out_shape=jax.ShapeDtypeStruct((b, pairs, t, LANES), BF16),
        scratch_shapes=[pltpu.VMEM((heads // pairs, tq, t), F32)],
        compiler_params=_vmem_params(("parallel", "arbitrary"), 48),
        name="mla",
    )(q, k, v)


def _merge_kernel(on_ref, om_ref, gm_ref, x_ref, wbn_ref, wbm_ref, wo_ref, gpost_ref, gpre_ref, x1_out, h_out):
    a = _dot(on_ref[...], wbn_ref[...])
    b = _dot(om_ref[0, 0], wbm_ref[0])
    for pr in range(1, om_ref.shape[1]):
        b = b + _dot(om_ref[0, pr], wbm_ref[pr])
    merged = gm_ref[:, 0:D_MODEL].astype(F32) * a + gm_ref[:, D_MODEL:2 * D_MODEL].astype(F32) * b
    y = _dot(merged.astype(BF16), wo_ref[...])
    x1 = x_ref[...] + _rms(y, gpost_ref[...])
    x1_out[...] = x1
    h_out[...] = _rms(x1, gpre_ref[...]).astype(BF16)


def _merge_call(o_nsa, o_mla, gm, x2d, wbn, wbm, wo, g_post, g_pre_ffn):
    n = x2d.shape[0]
    rows = MERGE_ROWS
    _, pairs, seq, _ = o_mla.shape
    per_seq = seq // rows

    def row_spec(width):
        return pl.BlockSpec((rows, width), lambda i: (i, 0))

    mla_spec = pl.BlockSpec((1, pairs, rows, LANES), lambda i: (i // per_seq, 0, i % per_seq, 0))
    consts = (wbn, wbm, wo, g_post, g_pre_ffn)
    return pl.pallas_call(
        _merge_kernel,
        grid=(n // rows,),
        in_specs=[row_spec(NSA_WIDTH), mla_spec, row_spec(2 * D_MODEL), row_spec(D_MODEL)]
        + [_const_spec(c.shape) for c in consts],
        out_specs=[row_spec(D_MODEL), row_spec(D_MODEL)],
        out_shape=[jax.ShapeDtypeStruct((n, D_MODEL), F32), jax.ShapeDtypeStruct((n, D_MODEL), BF16)],
        compiler_params=_vmem_params(("parallel",), 40),
        name="merge",
    )(o_nsa, o_mla, gm, x2d, *consts)


def _ffn_kernel(h_ref, x1_ref, wup_ref, wconv_ref, bconv_ref, wdown_ref, g_ref, x2_out, halo_ref, sa_ref, sv_ref,
                acc_ref, *, tiles_per_seq):
    rows = h_ref.shape[0]
    pad = SUBLANES
    keep = CONV_WIDTH - 1
    seq_start = pl.program_id(0) % tiles_per_seq == 0
    h = h_ref[...]

    @pl.when(pl.program_id(0) == 0)
    def _():
        halo_ref[...] = jnp.zeros_like(halo_ref)

    def conv_chunk(off, s_ref):
        cols = slice(off, off + FFN_CHUNK)
        u = _dot(h, wup_ref[:, cols])
        prev = jnp.where(seq_start, 0.0, halo_ref[pad - keep:pad, cols])
        s_ref[pad - keep:pad, :] = prev
        s_ref[pad:pad + rows, :] = u
        halo_ref[pad - keep:pad, cols] = s_ref[pad + rows - keep:pad + rows, :]
        y = wconv_ref[CONV_WIDTH - 1:CONV_WIDTH, cols] * u + bconv_ref[:, cols]
        for k in range(CONV_WIDTH - 1):
            shift = CONV_WIDTH - 1 - k
            y = y + wconv_ref[k:k + 1, cols] * s_ref[pad - shift:pad - shift + rows, :]
        return y

    for c in range(D_FF // FFN_CHUNK):
        a = conv_chunk(c * FFN_CHUNK, sa_ref)
        v = conv_chunk(D_FF + c * FFN_CHUNK, sv_ref)
        part = _dot((_gelu_tanh(a) * v).astype(BF16), wdown_ref[c * FFN_CHUNK:(c + 1) * FFN_CHUNK, :])
        if c == 0:
            acc_ref[...] = part
        else:
            acc_ref[...] += part
    x2_out[...] = x1_ref[...] + _rms(acc_ref[...], g_ref[...])


def _ffn_call(h2, x1, wup, wconv, bconv, wdown, g_post_ffn, seq):
    n = x1.shape[0]
    rows = FFN_ROWS

    def row_spec():
        return pl.BlockSpec((rows, D_MODEL), lambda i: (i, 0))

    consts = (wup, wconv, bconv, wdown, g_post_ffn)
    return pl.pallas_call(
        functools.partial(_ffn_kernel, tiles_per_seq=seq // rows),
        grid=(n // rows,),
        in_specs=[row_spec(), row_spec()] + [_const_spec(c.shape) for c in consts],
        out_specs=row_spec(),
        out_shape=jax.ShapeDtypeStruct((n, D_MODEL), F32),
        scratch_shapes=[pltpu.VMEM((SUBLANES, 2 * D_FF), F32),
                        pltpu.VMEM((rows + SUBLANES, FFN_CHUNK), F32),
                        pltpu.VMEM((rows + SUBLANES, FFN_CHUNK), F32),
                        pltpu.VMEM((rows, D_MODEL), F32)],
        compiler_params=_vmem_params(("arbitrary",), 48),
        name="ffn",
    )(h2, x1, *consts)


def _ple_kernel(p_ref, x_ref, wple_ref, wgate_ref, g_ref, out_ref):
    x = x_ref[...]
    e = _dot(p_ref[...].astype(BF16), wple_ref[...])
    gate = jax.nn.sigmoid(_dot(x.astype(BF16), wgate_ref[...]))
    out_ref[...] = x + _rms(e * gate, g_ref[...])


def _ple_call(p2d, x2, wple, wgate, g_ple):
    n = x2.shape[0]
    rows = PLE_ROWS
    consts = (wple, wgate, g_ple)
    return pl.pallas_call(
        _ple_kernel,
        grid=(n // rows,),
        in_specs=[pl.BlockSpec((rows, PLE_DIM), lambda i: (i, 0)), pl.BlockSpec((rows, D_MODEL), lambda i: (i, 0))]
        + [_const_spec(c.shape) for c in consts],
        out_specs=pl.BlockSpec((rows, D_MODEL), lambda i: (i, 0)),
        out_shape=jax.ShapeDtypeStruct((n, D_MODEL), F32),
        compiler_params=_vmem_params(("parallel",), 32),
        name="ple",
    )(p2d, x2, *consts)


def _pad_heads(w, heads, width):
    rows = w.shape[0]
    w = w.reshape(rows, heads, width)
    return jnp.pad(w, ((0, 0), (0, 0), (0, LANES - width))).reshape(rows, heads * LANES)


def _swap_halves(w):
    half = w.shape[-1] // 2
    return jnp.concatenate([w[..., half:], w[..., :half]], axis=-1)


def _layout_w_in(w_in):
    sizes = (NSA_WIDTH, 6 * NSA_KV_GROUPS * NSA_HEAD_DIM, 3 * NSA_HEADS, Q_LORA, KV_LORA, QK_ROPE, 2 * D_MODEL)
    offs = np.cumsum((0,) + sizes)
    wq, wkv, wgate, wcq, wckv, wkr, wmerge = (w_in[:, offs[i]:offs[i + 1]] for i in range(len(sizes)))
    rows = w_in.shape[0]
    wkv = wkv.reshape(rows, 6, NSA_KV_GROUPS * NSA_HEAD_DIM)
    small = jnp.concatenate([wgate, jnp.zeros((rows, _SMALL_KR - wgate.shape[1]), w_in.dtype), wkr, _swap_halves(wkr),
                             jnp.zeros((rows, LANES - _SMALL_KR - 2 * QK_ROPE), w_in.dtype)], axis=1)
    parts = [_pad_heads(wq * (NSA_HEAD_DIM ** -0.5), NSA_HEADS, NSA_HEAD_DIM), wkv[:, 0], wkv[:, 1]]
    parts += [_pad_heads(wkv[:, i], NSA_KV_GROUPS, NSA_HEAD_DIM) for i in range(2, 6)]
    parts += [small, wcq, wckv, wmerge]
    w_all = jnp.concatenate(parts, axis=1).astype(BF16)
    assert w_all.shape[1] == _C_END
    return w_all


def _layout_compress(w1, pos, b1, w2):
    dh, g, half_tok = NSA_HEAD_DIM, NSA_KV_GROUPS, CMP_BLOCK // 2
    eye = jnp.eye(g, dtype=w1.dtype)
    w1 = w1.reshape(CMP_BLOCK, dh, dh)
    w1e = jnp.einsum("lde,gh->lgdhe", w1, eye).reshape(CMP_BLOCK * g * dh, g * dh)
    pose = jnp.broadcast_to(pos[:, None, :], (CMP_BLOCK, g, dh)).reshape(1, CMP_BLOCK * g * dh)
    b1e = jnp.tile(b1, g).reshape(1, g * dh)
    w2e = jnp.einsum("de,gh->gdhe", w2, eye)
    w2e = jnp.pad(w2e, ((0, 0), (0, 0), (0, 0), (0, LANES - dh))).reshape(g * dh, g * LANES)
    del half_tok
    return w1e.astype(BF16), pose.astype(F32), b1e.astype(F32), w2e.astype(BF16)


def _position_tables(seq):
    t = np.arange(seq)
    hi = (t // SEL_BLOCK) * SEL_BLOCK
    lo = t % SEL_BLOCK
    qtab = np.zeros((seq, LANES), np.float32)
    qtab[:, FEAT + 0] = -hi
    qtab[:, FEAT + 1] = -lo
    qtab[:, FEAT + 2] = 1.0
    qtab[:, FEAT + 3] = 1.0
    ktab = np.zeros((seq, LANES), np.float32)
    ktab[:, FEAT + 0] = 1.0
    ktab[:, FEAT + 1] = 1.0
    ktab[:, FEAT + 2] = hi
    ktab[:, FEAT + 3] = lo
    ktab[t, SEL_LANE0 + t // SEL_BLOCK] = 1.0
    per = SEL_BLOCK // CMP_STRIDE
    n_sb = seq // SEL_BLOCK
    r = np.arange(per * n_sb)
    end = (per * (r % n_sb) + r // n_sb) * CMP_STRIDE + CMP_BLOCK - 1
    ctab = np.zeros((per * n_sb, NSA_KV_GROUPS, LANES), np.float32)
    ctab[:, :, FEAT + 0] = 1.0
    ctab[:, :, FEAT + 1] = 1.0
    ctab[:, :, FEAT + 2] = ((end // SEL_BLOCK) * SEL_BLOCK)[:, None]
    ctab[:, :, FEAT + 3] = (end % SEL_BLOCK)[:, None]
    return jnp.asarray(qtab), jnp.asarray(ktab), jnp.asarray(ctab.reshape(per * n_sb, NSA_KV_GROUPS * LANES))


def _rope_table(positions):
    inv_freq = ROPE_THETA ** (-jnp.arange(0, QK_ROPE, 2, dtype=F32) / QK_ROPE)
    ang = positions.astype(F32)[..., None] * inv_freq
    cos, sin = jnp.cos(ang), jnp.sin(ang)
    c2 = jnp.concatenate([cos, cos], axis=-1)
    s2 = jnp.concatenate([-sin, sin], axis=-1)
    scale = (QK_NOPE + QK_ROPE) ** -0.5
    lead = c2.shape[:-1]
    mq = jnp.concatenate([jnp.full(lead + (QK_NOPE,), scale, F32), scale * c2, scale * s2], axis=-1)
    rk = jnp.concatenate([jnp.zeros(lead + (_SMALL_KR,), F32), c2, s2,
                          jnp.zeros(lead + (LANES - _SMALL_KR - 2 * QK_ROPE,), F32)], axis=-1)
    return jnp.concatenate([mq, rk], axis=-1).reshape(-1, 2 * LANES)


def _rope_placement():
    place = np.zeros((LANES, MLA_HEADS, LANES), np.float32)
    i = np.arange(QK_ROPE)
    place[_SMALL_KR + i, :, QK_NOPE + i] = 1.0
    place[_SMALL_KR + i, :, QK_NOPE + QK_ROPE + i] = 1.0
    return jnp.asarray(place.reshape(LANES, MLA_HEADS * LANES), dtype=BF16)


def kernel(x, p, positions, g_pre_mix, w_in, nsa_pos_k, nsa_pos_v, nsa_ck_w1, nsa_ck_b1, nsa_ck_w2, nsa_cv_w1, nsa_cv_b1, nsa_cv_w2, mla_g_q, mla_w_uq, mla_g_kv, mla_w_ukv, w_br_nsa, w_br_mla, w_o, g_post_mix, g_pre_ffn, w_up, w_conv, b_conv, w_down, g_post_ffn, w_ple, w_ple_gate, g_ple):
    b, t, d = x.shape
    depth = w_in.shape[0]
    assert d == D_MODEL and t // SEL_BLOCK == LANES - SEL_LANE0 and t % PROJ_ROWS == 0
    n = b * t
    qtab, ktab, ctab = _position_tables(t)
    ropetab = _rope_table(positions)
    place = _rope_placement()
    xc = x.reshape(n, d)
    for i in range(depth):
        w_all = _layout_w_in(w_in[i])
        wuq = mla_w_uq[i].reshape(Q_LORA, MLA_HEADS, QK_NOPE + QK_ROPE)
        wuq = jnp.concatenate([wuq, _swap_halves(wuq[..., QK_NOPE:])], axis=-1).reshape(Q_LORA, MLA_HEADS * LANES)
        wukv = mla_w_ukv[i].reshape(KV_LORA, MLA_HEADS, QK_NOPE + V_DIM)
        wk = _pad_heads(wukv[..., :QK_NOPE].reshape(KV_LORA, -1), MLA_HEADS, QK_NOPE)
        wv = _pad_heads(wukv[..., QK_NOPE:].reshape(KV_LORA, -1), MLA_HEADS, V_DIM)
        (q_n, kc, vc, ks, vs, kw, vw, gs, q_m, k_m, v_m, gm) = _proj_call(
            xc, g_pre_mix[i][None], w_all, wuq.astype(BF16), mla_g_q[i][None], wk.astype(BF16), wv.astype(BF16),
            mla_g_kv[i][None], place, qtab, ktab, ropetab, t)

        n_sb = t // SEL_BLOCK
        kcmp, vcmp = _compress_call(
            kc.reshape(b, n_sb, SEL_BLOCK * LANES), vc.reshape(b, n_sb, SEL_BLOCK * LANES),
            *_layout_compress(nsa_ck_w1[i], nsa_pos_k[i], nsa_ck_b1[i], nsa_ck_w2[i]),
            *_layout_compress(nsa_cv_w1[i], nsa_pos_v[i], nsa_cv_b1[i], nsa_cv_w2[i]), ctab)

        def seq3(a):
            return a.reshape(b, t, a.shape[-1])

        o_nsa = _nsa_call(seq3(q_n), kcmp, vcmp, seq3(ks), seq3(vs), seq3(kw), seq3(vw), seq3(gs))
        o_mla = _mla_call(q_m, k_m, v_m)

        x1, h2 = _merge_call(o_nsa.reshape(n, NSA_WIDTH), o_mla, gm, xc, w_br_nsa[i].astype(BF16),
                             w_br_mla[i].astype(BF16).reshape(-1, LANES, D_MODEL), w_o[i].astype(BF16),
                             g_post_mix[i][None], g_pre_ffn[i][None])
        x2 = _ffn_call(h2, x1, w_up[i].astype(BF16), w_conv[i], b_conv[i][None], w_down[i].astype(BF16),
                       g_post_ffn[i][None], t)
        xc = _ple_call(p[i].reshape(n, PLE_DIM), x2, w_ple[i].astype(BF16), w_ple_gate[i].astype(BF16),
                       g_ple[i][None])
    return xc.reshape(b, t, d)
```

```python
import functools
import math

import numpy as np
import jax
import jax.numpy as jnp
from jax import lax
from jax.experimental import pallas as pl
from jax.experimental.pallas import tpu as pltpu

F32 = jnp.float32
BF16 = jnp.bfloat16

D_MODEL = 1024
PLE_DIM = 256
NSA_HEADS = 8
NSA_KV_GROUPS = 2
NSA_REP = NSA_HEADS // NSA_KV_GROUPS
NSA_HEAD_DIM = 64
NSA_WIDTH = NSA_HEADS * NSA_HEAD_DIM
CMP_BLOCK = 32
CMP_STRIDE = 16
SEL_BLOCK = 64
N_SEL = 16
WINDOW = 512
MLA_HEADS = 8
Q_LORA = 256
KV_LORA = 128
QK_NOPE = 64
QK_ROPE = 32
V_DIM = 64
MLA_WIDTH = MLA_HEADS * V_DIM
ROPE_THETA = 10000.0
D_FF = 2816
CONV_WIDTH = 3
EPS = 1e-6
NEG = -1e30
BIG = 1e9
ALIBI_SLOPES = tuple(2.0 ** (-8.0 * (i + 1) / NSA_HEADS) for i in range(NSA_HEADS))

LANES = 128
SUBLANES = 8

FEAT = NSA_HEAD_DIM
SEL_LANE0 = 96
ONES_LANE = NSA_HEAD_DIM

PROJ_ROWS = 512
NSA_TQ = 128
NSA_SEL_SLAB = 512
MLA_TQ = 256
MLA_SLAB = 512
MERGE_ROWS = 512
FFN_ROWS = 512
FFN_CHUNK = 256
PLE_ROWS = 512

_C_Q = 0
_C_KC = _C_Q + NSA_HEADS * LANES
_C_VC = _C_KC + LANES
_C_KS = _C_VC + LANES
_C_VS = _C_KS + NSA_KV_GROUPS * LANES
_C_KW = _C_VS + NSA_KV_GROUPS * LANES
_C_VW = _C_KW + NSA_KV_GROUPS * LANES
_C_SMALL = _C_VW + NSA_KV_GROUPS * LANES
_C_CQ = _C_SMALL + LANES
_C_CKV = _C_CQ + Q_LORA
_C_MERGE = _C_CKV + KV_LORA
_C_END = _C_MERGE + 2 * D_MODEL
_SMALL_KR = 32


def _vmem_params(semantics, mib):
    return pltpu.CompilerParams(dimension_semantics=semantics, vmem_limit_bytes=mib * 1024 * 1024)


def _const_spec(shape):
    nd = len(shape)
    return pl.BlockSpec(shape, lambda *_: (0,) * nd, pipeline_mode=pl.Buffered(1))


def _rms(x, g):
    return x * lax.rsqrt(jnp.mean(x * x, axis=-1, keepdims=True) + EPS) * g


def _gelu_tanh(x):
    return 0.5 * x * (1.0 + jnp.tanh(math.sqrt(2.0 / math.pi) * (x + 0.044715 * (x * x * x))))


def _dot(a, b):
    return jnp.dot(a, b, preferred_element_type=F32)


def _dot_nt(a, b):
    return lax.dot_general(a, b, (((1,), (1,)), ((), ())), preferred_element_type=F32)


def _dot_tn(a, b):
    return lax.dot_general(a, b, (((0,), (0,)), ((), ())), preferred_element_type=F32)


def _proj_kernel(x_ref, g_ref, w_ref, wuq_ref, gq_ref, wk_ref, wv_ref, gkv_ref, place_ref, qtab_ref, ktab_ref,
                 rope_ref, q_out, kc_out, vc_out, ks_out, vs_out, kw_out, vw_out, gs_out, qm_out, km_out,
                 vm_out, gm_out):
    rows = x_ref.shape[0]
    h = _rms(x_ref[...], g_ref[...]).astype(BF16)

    def proj(a, b):
        return _dot(h, w_ref[:, a:b])

    lane = lax.broadcasted_iota(jnp.int32, (rows, LANES), 1)
    ones_col = (lane == ONES_LANE).astype(F32)

    qtab = qtab_ref[...]
    for hd in range(NSA_HEADS):
        a = _C_Q + hd * LANES
        q_out[:, hd * LANES:(hd + 1) * LANES] = (proj(a, a + LANES) + ALIBI_SLOPES[hd] * qtab).astype(BF16)

    kc_out[...] = proj(_C_KC, _C_KC + LANES).astype(BF16)
    vc_out[...] = proj(_C_VC, _C_VC + LANES).astype(BF16)

    ktab_sel = ktab_ref[...]
    ktab_win = jnp.where(lane < SEL_LANE0, ktab_sel, 0.0)
    for g in range(NSA_KV_GROUPS):
        sl = slice(g * LANES, (g + 1) * LANES)
        ks_out[:, sl] = (proj(_C_KS + g * LANES, _C_KS + (g + 1) * LANES) + ktab_sel).astype(BF16)
        vs_out[:, sl] = (proj(_C_VS + g * LANES, _C_VS + (g + 1) * LANES) + ones_col).astype(BF16)
        kw_out[:, sl] = (proj(_C_KW + g * LANES, _C_KW + (g + 1) * LANES) + ktab_win).astype(BF16)
        vw_out[:, sl] = (proj(_C_VW + g * LANES, _C_VW + (g + 1) * LANES) + ones_col).astype(BF16)

    small = proj(_C_SMALL, _C_SMALL + LANES)
    gs_out[...] = jax.nn.sigmoid(small).astype(BF16)

    mq = rope_ref[:, 0:LANES]
    rk = rope_ref[:, LANES:2 * LANES]
    t = small * rk
    k_rope = (t + pltpu.roll(t, LANES - QK_ROPE, 1)).astype(BF16)

    cqn = _rms(proj(_C_CQ, _C_CQ + Q_LORA), gq_ref[...]).astype(BF16)
    for hd in range(MLA_HEADS):
        sl = slice(hd * LANES, (hd + 1) * LANES)
        qm_out[0, hd] = (_dot(cqn, wuq_ref[:, sl]) * mq).astype(BF16)

    ckvn = _rms(proj(_C_CKV, _C_CKV + KV_LORA), gkv_ref[...]).astype(BF16)
    for hd in range(MLA_HEADS):
        sl = slice(hd * LANES, (hd + 1) * LANES)
        km_out[0, hd] = (_dot(ckvn, wk_ref[:, sl]) + _dot(k_rope, place_ref[:, sl])).astype(BF16)
        vm_out[0, hd] = (_dot(ckvn, wv_ref[:, sl]) + ones_col).astype(BF16)

    half = D_MODEL // 2
    for c in range(2 * D_MODEL // half):
        a = _C_MERGE + c * half
        gm_out[:, c * half:(c + 1) * half] = jax.nn.sigmoid(proj(a, a + half)).astype(BF16)


def _proj_call(x2d, g_pre, w_all, wuq, g_q, wk, wv, g_kv, place, qtab, ktab, ropetab, seq):
    n = x2d.shape[0]
    rows = PROJ_ROWS
    per_seq = seq // rows

    def row_spec(width):
        return pl.BlockSpec((rows, width), lambda i: (i, 0))

    tab_spec = pl.BlockSpec((rows, LANES), lambda i: (i % per_seq, 0))
    head_spec = pl.BlockSpec((1, MLA_HEADS, rows, LANES), lambda i: (i // per_seq, 0, i % per_seq, 0))
    head_shape = jax.ShapeDtypeStruct((n // seq, MLA_HEADS, seq, LANES), BF16)
    widths = (NSA_HEADS * LANES, LANES, LANES, NSA_KV_GROUPS * LANES, NSA_KV_GROUPS * LANES,
              NSA_KV_GROUPS * LANES, NSA_KV_GROUPS * LANES, LANES)
    return pl.pallas_call(
        _proj_kernel,
        grid=(n // rows,),
        in_specs=[row_spec(D_MODEL), _const_spec(g_pre.shape), _const_spec(w_all.shape), _const_spec(wuq.shape),
                  _const_spec(g_q.shape), _const_spec(wk.shape), _const_spec(wv.shape), _const_spec(g_kv.shape),
                  _const_spec(place.shape), tab_spec, tab_spec, row_spec(2 * LANES)],
        out_specs=[row_spec(w) for w in widths] + [head_spec] * 3 + [row_spec(2 * D_MODEL)],
        out_shape=[jax.ShapeDtypeStruct((n, w), BF16) for w in widths] + [head_shape] * 3
        + [jax.ShapeDtypeStruct((n, 2 * D_MODEL), BF16)],
        compiler_params=_vmem_params(("parallel",), 48),
        name="proj",
    )(x2d, g_pre, w_all, wuq, g_q, wk, wv, g_kv, place, qtab, ktab, ropetab)


def _compress_kernel(kx_ref, vx_ref, w1k_ref, posk_ref, b1k_ref, w2k_ref, w1v_ref, posv_ref, b1v_ref, w2v_ref,
                     ktab_ref, kc_out, vc_out):
    n_sb = kx_ref.shape[1]
    half = w1k_ref.shape[0] // 2
    per = SEL_BLOCK // CMP_STRIDE

    def one(x_ref, w1_ref, pos_ref, b1_ref, w2_ref):
        x = x_ref[0].astype(F32)
        tops, bots = [], []
        for c in range(per):
            xc = x[:, c * half:(c + 1) * half]
            tops.append(_dot((xc + pos_ref[:, 0:half]).astype(BF16), w1_ref[0:half, :]))
            bots.append(_dot((xc + pos_ref[:, half:2 * half]).astype(BF16), w1_ref[half:2 * half, :]))
        nxt = pltpu.roll(bots[0], n_sb - 1, 0)
        row = lax.broadcasted_iota(jnp.int32, nxt.shape, 0)
        nxt = jnp.where(row == n_sb - 1, 0.0, nxt)
        pre = jnp.concatenate([tops[c] + (bots[c + 1] if c + 1 < per else nxt) for c in range(per)], axis=0)
        return _dot(_gelu_tanh(pre + b1_ref[...]).astype(BF16), w2_ref[...])

    kc_out[0] = (one(kx_ref, w1k_ref, posk_ref, b1k_ref, w2k_ref) + ktab_ref[...]).astype(BF16)
    vc_out[0] = one(vx_ref, w1v_ref, posv_ref, b1v_ref, w2v_ref).astype(BF16)


def _compress_call(kx, vx, w1k, posk, b1k, w2k, w1v, posv, b1v, w2v, ktab):
    b, n_sb, width = kx.shape
    n_rows = n_sb * (SEL_BLOCK // CMP_STRIDE)
    x_spec = pl.BlockSpec((1, n_sb, width), lambda i: (i, 0, 0))
    o_spec = pl.BlockSpec((1, n_rows, NSA_KV_GROUPS * LANES), lambda i: (i, 0, 0))
    consts = (w1k, posk, b1k, w2k, w1v, posv, b1v, w2v, ktab)
    return pl.pallas_call(
        _compress_kernel,
        grid=(b,),
        in_specs=[x_spec, x_spec] + [_const_spec(c.shape) for c in consts],
        out_specs=[o_spec, o_spec],
        out_shape=[jax.ShapeDtypeStruct((b, n_rows, NSA_KV_GROUPS * LANES), BF16)] * 2,
        compiler_params=_vmem_params(("parallel",), 32),
        name="compress",
    )(kx, vx, *consts)


def _normalize(acc):
    return acc * (1.0 / acc[:, ONES_LANE:ONES_LANE + 1])


def _dense_attend(q, tiles, k_at, v_at, s_ref):
    maxes = []
    off = 0
    for start, width, mask in tiles:
        s = _dot_nt(q, k_at(start, width))
        if mask is not None:
            s = jnp.where(mask, s, NEG)
        s_ref[:, off:off + width] = s
        maxes.append(jnp.max(s, axis=1, keepdims=True))
        off += width
    m = functools.reduce(jnp.maximum, maxes)
    acc = None
    off = 0
    for start, width, _ in tiles:
        part = _dot(jnp.exp(s_ref[:, off:off + width] - m).astype(BF16), v_at(start, width))
        acc = part if acc is None else acc + part
        off += width
    return _normalize(acc)


def _split_tiles(start, total, width):
    return [(start + o, min(width, total - o), None) for o in range(0, total, width)]


def _nsa_kernel(q_ref, kcmp_ref, vcmp_ref, ks_ref, vs_ref, kw_ref, vw_ref, gs_ref, o_ref, qsel_ref, ocmp_ref,
                osel_ref, owin_ref, s_ref):
    tq = q_ref.shape[1]
    seq = ks_ref.shape[1]
    qi = pl.program_id(1)
    t0 = qi * tq
    m_rows = NSA_REP * tq
    n_cmp_rows = kcmp_ref.shape[1]
    n_sb = n_cmp_rows // (SEL_BLOCK // CMP_STRIDE)
    per = SEL_BLOCK // CMP_STRIDE
    win_tiles = WINDOW // tq
    groups = [slice(g * LANES, (g + 1) * LANES) for g in range(NSA_KV_GROUPS)]

    def q_group(g):
        return jnp.concatenate(
            [q_ref[0, :, (g * NSA_REP + r) * LANES:(g * NSA_REP + r + 1) * LANES] for r in range(NSA_REP)], axis=0)

    def rel(width):
        return (lax.broadcasted_iota(jnp.int32, (m_rows, width), 1)
                - lax.broadcasted_iota(jnp.int32, (m_rows, width), 0) % tq)

    lane_q = lax.broadcasted_iota(jnp.int32, (m_rows, LANES), 1)
    for g, gl in enumerate(groups):
        qg = q_group(g)

        s_t = _dot_nt(kcmp_ref[0, :, gl], qg)
        krow = lax.broadcasted_iota(jnp.int32, s_t.shape, 0)
        blk_end = (per * (krow % n_sb) + krow // n_sb) * CMP_STRIDE + (CMP_BLOCK - 1)
        t_abs = t0 + lax.broadcasted_iota(jnp.int32, s_t.shape, 1) % tq
        cmask = t_abs >= blk_end
        s_t = jnp.where(cmask, s_t, NEG)
        e = jnp.exp(s_t - jnp.max(s_t, axis=0, keepdims=True))
        p_t = jnp.where(cmask, e, 0.0) * (1.0 / jnp.sum(e, axis=0, keepdims=True))
        o_cmp = _dot_tn(p_t.astype(BF16), vcmp_ref[0, :, gl])

        imp = None
        for r in range(NSA_REP):
            for c in range(per):
                piece = p_t[c * n_sb:(c + 1) * n_sb, r * tq:(r + 1) * tq]
                imp = piece if imp is None else imp + piece
        sb = lax.broadcasted_iota(jnp.int32, (n_sb, tq), 0)
        cur = (t0 + lax.broadcasted_iota(jnp.int32, (n_sb, tq), 1)) // SEL_BLOCK
        forced = (sb == 0) | (sb == cur) | (sb == cur - 1)
        score = jnp.where(forced, BIG, jnp.where(sb > cur, -BIG, imp))
        rank = jnp.zeros((n_sb, tq), jnp.int32)
        for j in range(n_sb):
            sj = score[j:j + 1, :]
            before = (sj > score) | ((sj == score) & (sb > j))
            rank = rank + before.astype(jnp.int32)
        sel_neg = jnp.where(rank < min(N_SEL, n_sb), 0.0, NEG)
        sel_q = jnp.concatenate([jnp.zeros((SEL_LANE0, tq), F32), sel_neg], axis=0).T.astype(BF16)
        qsel_ref[g] = jnp.where(lane_q >= SEL_LANE0, jnp.concatenate([sel_q] * NSA_REP, axis=0), qg)
        ocmp_ref[g] = o_cmp

    def kv_at(k_ref, gl):
        return lambda start, width: k_ref[0, pl.ds(start, width), gl]

    slab = NSA_SEL_SLAB
    for n in range(seq // slab):

        @pl.when(qi // (slab // tq) == n)
        def _(n=n):
            last = rel(slab) <= t0 - n * slab
            tiles = _split_tiles(0, n * slab, slab) + [(n * slab, slab, last)]
            for g, gl in enumerate(groups):
                osel_ref[g] = _dense_attend(qsel_ref[g], tiles, kv_at(ks_ref, gl), kv_at(vs_ref, gl), s_ref.at[g])

    for v in range(win_tiles + 1):

        @pl.when((qi == v) if v < win_tiles else (qi >= win_tiles))
        def _(v=v):
            causal = rel(tq) <= 0
            if v < win_tiles:
                tiles = _split_tiles(0, v * tq, WINDOW) + [(v * tq, tq, causal)]
            else:
                first = pl.multiple_of((qi - win_tiles) * tq, tq)
                tiles = [(first, tq, rel(tq) > 0), (pl.multiple_of(first + tq, tq), WINDOW - tq, None),
                         (pl.multiple_of(t0, tq), tq, causal)]
            for g, gl in enumerate(groups):
                owin_ref[g] = _dense_attend(q_group(g), tiles, kv_at(kw_ref, gl), kv_at(vw_ref, gl), s_ref.at[g])

    gates = gs_ref[0].astype(F32)
    outs = []
    for g in range(NSA_KV_GROUPS):
        for r in range(NSA_REP):
            hd = g * NSA_REP + r
            rs = slice(r * tq, (r + 1) * tq)
            o = (gates[:, hd:hd + 1] * ocmp_ref[g, rs, :]
                 + gates[:, NSA_HEADS + hd:NSA_HEADS + hd + 1] * osel_ref[g, rs, :]
                 + gates[:, 2 * NSA_HEADS + hd:2 * NSA_HEADS + hd + 1] * owin_ref[g, rs, :])
            outs.append(o[:, 0:NSA_HEAD_DIM])
    o_ref[0] = jnp.concatenate(outs, axis=1).astype(o_ref.dtype)


def _nsa_call(q, kcmp, vcmp, ks, vs, kw, vw, gs):
    b, t, _ = q.shape
    tq = NSA_TQ
    m_rows = NSA_REP * tq
    branch_out = pltpu.VMEM((NSA_KV_GROUPS, m_rows, LANES), F32)
    kv_spec = pl.BlockSpec((1, t, NSA_KV_GROUPS * LANES), lambda i, j: (i, 0, 0))
    cmp_spec = pl.BlockSpec((1, kcmp.shape[1], NSA_KV_GROUPS * LANES), lambda i, j: (i, 0, 0))
    return pl.pallas_call(
        _nsa_kernel,
        grid=(b, t // tq),
        in_specs=[pl.BlockSpec((1, tq, NSA_HEADS * LANES), lambda i, j: (i, j, 0)), cmp_spec, cmp_spec,
                  kv_spec, kv_spec, kv_spec, kv_spec, pl.BlockSpec((1, tq, LANES), lambda i, j: (i, j, 0))],
        out_specs=pl.BlockSpec((1, tq, NSA_WIDTH), lambda i, j: (i, j, 0)),
        out_shape=jax.ShapeDtypeStruct((b, t, NSA_WIDTH), BF16),
        scratch_shapes=[pltpu.VMEM((NSA_KV_GROUPS, m_rows, LANES), BF16), branch_out, branch_out, branch_out,
                        pltpu.VMEM((NSA_KV_GROUPS, m_rows, t), F32)],
        compiler_params=_vmem_params(("parallel", "arbitrary"), 48),
        name="nsa",
    )(q, kcmp, vcmp, ks, vs, kw, vw, gs)


def _mla_kernel(qlo_ref, qhi_ref, k_ref, v_ref, o_ref, s_ref):
    tq = qlo_ref.shape[2]
    nq = k_ref.shape[2] // tq
    step = pl.program_id(1)
    pairs = o_ref.shape[1]
    per_pair = MLA_HEADS // pairs

    for n in range(nq // 2):

        @pl.when(step == n)
        def _(n=n):
            causal = (lax.broadcasted_iota(jnp.int32, (tq, tq), 1) <= lax.broadcasted_iota(jnp.int32, (tq, tq), 0))

            def pair_body(pr, carry):
                for which, (q_ref, qt) in enumerate(((qlo_ref, n), (qhi_ref, nq - 1 - n))):
                    tiles = _split_tiles(0, qt * tq, MLA_SLAB) + [(qt * tq, tq, causal)]
                    outs = []
                    for e in range(per_pair):
                        hd = per_pair * pr + e
                        o = _dense_attend(
                            q_ref[0, hd], tiles,
                            lambda start, width, hd=hd: k_ref[0, hd, pl.ds(start, width), :],
                            lambda start, width, hd=hd: v_ref[0, hd, pl.ds(start, width), :],
                            s_ref.at[which * per_pair + e])
                        outs.append(o[:, 0:V_DIM])
                    o_ref[0, pr, qt * tq:(qt + 1) * tq, :] = jnp.concatenate(outs, axis=1).astype(o_ref.dtype)
                return carry

            lax.fori_loop(0, pairs, pair_body, 0)


def _mla_call(q, k, v):
    b, heads, t, width = q.shape
    tq = MLA_TQ
    nq = t // tq
    pairs = heads * V_DIM // LANES
    kv_spec = pl.BlockSpec((1, heads, t, width), lambda i, j: (i, 0, 0, 0))
    return pl.pallas_call(
        _mla_kernel,
        grid=(b, nq // 2),
        in_specs=[pl.BlockSpec((1, heads, tq, width), lambda i, j: (i, 0, j, 0)),
                  pl.BlockSpec((1, heads, tq, width), lambda i, j: (i, 0, nq - 1 - j, 0)), kv_spec, kv_spec],
        out_specs=pl.BlockSpec((1, pairs, t, LANES), lambda i, j: (i, 0, 0, 0)),
        out_shape=jax.ShapeDtypeStruct((b, pairs, t, LANES), BF16),
        scratch_shapes=[pltpu.VMEM((2 * (heads // pairs), tq, t), F32)],
        compiler_params=_vmem_params(("parallel", "arbitrary"), 56),
        name="mla",
    )(q, q, k, v)


def _merge_kernel(on_ref, om_ref, gm_ref, x_ref, wbn_ref, wbm_ref, wo_ref, gpost_ref, gpre_ref, x1_out, h_out):
    a = _dot(on_ref[...], wbn_ref[...])
    b = _dot(om_ref[0, 0], wbm_ref[0])
    for pr in range(1, om_ref.shape[1]):
        b = b + _dot(om_ref[0, pr], wbm_ref[pr])
    merged = gm_ref[:, 0:D_MODEL].astype(F32) * a + gm_ref[:, D_MODEL:2 * D_MODEL].astype(F32) * b
    y = _dot(merged.astype(BF16), wo_ref[...])
    x1 = x_ref[...] + _rms(y, gpost_ref[...])
    x1_out[...] = x1
    h_out[...] = _rms(x1, gpre_ref[...]).astype(BF16)


def _merge_call(o_nsa, o_mla, gm, x2d, wbn, wbm, wo, g_post, g_pre_ffn):
    n = x2d.shape[0]
    rows = MERGE_ROWS
    _, pairs, seq, _ = o_mla.shape
    per_seq = seq // rows

    def row_spec(width):
        return pl.BlockSpec((rows, width), lambda i: (i, 0))

    mla_spec = pl.BlockSpec((1, pairs, rows, LANES), lambda i: (i // per_seq, 0, i % per_seq, 0))
    consts = (wbn, wbm, wo, g_post, g_pre_ffn)
    return pl.pallas_call(
        _merge_kernel,
        grid=(n // rows,),
        in_specs=[row_spec(NSA_WIDTH), mla_spec, row_spec(2 * D_MODEL), row_spec(D_MODEL)]
        + [_const_spec(c.shape) for c in consts],
        out_specs=[row_spec(D_MODEL), row_spec(D_MODEL)],
        out_shape=[jax.ShapeDtypeStruct((n, D_MODEL), F32), jax.ShapeDtypeStruct((n, D_MODEL), BF16)],
        compiler_params=_vmem_params(("parallel",), 40),
        name="merge",
    )(o_nsa, o_mla, gm, x2d, *consts)


def _ffn_kernel(h_ref, x1_ref, wup_ref, wconv_ref, bconv_ref, wdown_ref, g_ref, x2_out, halo_ref, sa_ref, sv_ref,
                acc_ref, *, tiles_per_seq):
    rows = h_ref.shape[0]
    pad = SUBLANES
    keep = CONV_WIDTH - 1
    seq_start = pl.program_id(0) % tiles_per_seq == 0
    h = h_ref[...]

    @pl.when(pl.program_id(0) == 0)
    def _():
        halo_ref[...] = jnp.zeros_like(halo_ref)

    def conv_chunk(off, s_ref):
        cols = slice(off, off + FFN_CHUNK)
        u = _dot(h, wup_ref[:, cols])
        prev = jnp.where(seq_start, 0.0, halo_ref[pad - keep:pad, cols])
        s_ref[pad - keep:pad, :] = prev
        s_ref[pad:pad + rows, :] = u
        halo_ref[pad - keep:pad, cols] = s_ref[pad + rows - keep:pad + rows, :]
        y = wconv_ref[CONV_WIDTH - 1:CONV_WIDTH, cols] * u + bconv_ref[:, cols]
        for k in range(CONV_WIDTH - 1):
            shift = CONV_WIDTH - 1 - k
            y = y + wconv_ref[k:k + 1, cols] * s_ref[pad - shift:pad - shift + rows, :]
        return y

    for c in range(D_FF // FFN_CHUNK):
        a = conv_chunk(c * FFN_CHUNK, sa_ref)
        v = conv_chunk(D_FF + c * FFN_CHUNK, sv_ref)
        part = _dot((_gelu_tanh(a) * v).astype(BF16), wdown_ref[c * FFN_CHUNK:(c + 1) * FFN_CHUNK, :])
        if c == 0:
            acc_ref[...] = part
        else:
            acc_ref[...] += part
    x2_out[...] = x1_ref[...] + _rms(acc_ref[...], g_ref[...])


def _ffn_call(h2, x1, wup, wconv, bconv, wdown, g_post_ffn, seq):
    n = x1.shape[0]
    rows = FFN_ROWS

    def row_spec():
        return pl.BlockSpec((rows, D_MODEL), lambda i: (i, 0))

    consts = (wup, wconv, bconv, wdown, g_post_ffn)
    return pl.pallas_call(
        functools.partial(_ffn_kernel, tiles_per_seq=seq // rows),
        grid=(n // rows,),
        in_specs=[row_spec(), row_spec()] + [_const_spec(c.shape) for c in consts],
        out_specs=row_spec(),
        out_shape=jax.ShapeDtypeStruct((n, D_MODEL), F32),
        scratch_shapes=[pltpu.VMEM((SUBLANES, 2 * D_FF), F32),
                        pltpu.VMEM((rows + SUBLANES, FFN_CHUNK), F32),
                        pltpu.VMEM((rows + SUBLANES, FFN_CHUNK), F32),
                        pltpu.VMEM((rows, D_MODEL), F32)],
        compiler_params=_vmem_params(("arbitrary",), 48),
        name="ffn",
    )(h2, x1, *consts)


def _ple_kernel(p_ref, x_ref, wple_ref, wgate_ref, g_ref, out_ref):
    x = x_ref[...]
    e = _dot(p_ref[...].astype(BF16), wple_ref[...])
    gate = jax.nn.sigmoid(_dot(x.astype(BF16), wgate_ref[...]))
    out_ref[...] = x + _rms(e * gate, g_ref[...])


def _ple_call(p2d, x2, wple, wgate, g_ple):
    n = x2.shape[0]
    rows = PLE_ROWS
    consts = (wple, wgate, g_ple)
    return pl.pallas_call(
        _ple_kernel,
        grid=(n // rows,),
        in_specs=[pl.BlockSpec((rows, PLE_DIM), lambda i: (i, 0)), pl.BlockSpec((rows, D_MODEL), lambda i: (i, 0))]
        + [_const_spec(c.shape) for c in consts],
        out_specs=pl.BlockSpec((rows, D_MODEL), lambda i: (i, 0)),
        out_shape=jax.ShapeDtypeStruct((n, D_MODEL), F32),
        compiler_params=_vmem_params(("parallel",), 32),
        name="ple",
    )(p2d, x2, *consts)


def _pad_heads(w, heads, width):
    rows = w.shape[0]
    w = w.reshape(rows, heads, width)
    return jnp.pad(w, ((0, 0), (0, 0), (0, LANES - width))).reshape(rows, heads * LANES)


def _swap_halves(w):
    half = w.shape[-1] // 2
    return jnp.concatenate([w[..., half:], w[..., :half]], axis=-1)


def _layout_w_in(w_in):
    sizes = (NSA_WIDTH, 6 * NSA_KV_GROUPS * NSA_HEAD_DIM, 3 * NSA_HEADS, Q_LORA, KV_LORA, QK_ROPE, 2 * D_MODEL)
    offs = np.cumsum((0,) + sizes)
    wq, wkv, wgate, wcq, wckv, wkr, wmerge = (w_in[:, offs[i]:offs[i + 1]] for i in range(len(sizes)))
    rows = w_in.shape[0]
    wkv = wkv.reshape(rows, 6, NSA_KV_GROUPS * NSA_HEAD_DIM)
    small = jnp.concatenate([wgate, jnp.zeros((rows, _SMALL_KR - wgate.shape[1]), w_in.dtype), wkr, _swap_halves(wkr),
                             jnp.zeros((rows, LANES - _SMALL_KR - 2 * QK_ROPE), w_in.dtype)], axis=1)
    parts = [_pad_heads(wq * (NSA_HEAD_DIM ** -0.5), NSA_HEADS, NSA_HEAD_DIM), wkv[:, 0], wkv[:, 1]]
    parts += [_pad_heads(wkv[:, i], NSA_KV_GROUPS, NSA_HEAD_DIM) for i in range(2, 6)]
    parts += [small, wcq, wckv, wmerge]
    w_all = jnp.concatenate(parts, axis=1).astype(BF16)
    assert w_all.shape[1] == _C_END
    return w_all


def _layout_compress(w1, pos, b1, w2):
    dh, g, half_tok = NSA_HEAD_DIM, NSA_KV_GROUPS, CMP_BLOCK // 2
    eye = jnp.eye(g, dtype=w1.dtype)
    w1 = w1.reshape(CMP_BLOCK, dh, dh)
    w1e = jnp.einsum("lde,gh->lgdhe", w1, eye).reshape(CMP_BLOCK * g * dh, g * dh)
    pose = jnp.broadcast_to(pos[:, None, :], (CMP_BLOCK, g, dh)).reshape(1, CMP_BLOCK * g * dh)
    b1e = jnp.tile(b1, g).reshape(1, g * dh)
    w2e = jnp.einsum("de,gh->gdhe", w2, eye)
    w2e = jnp.pad(w2e, ((0, 0), (0, 0), (0, 0), (0, LANES - dh))).reshape(g * dh, g * LANES)
    del half_tok
    return w1e.astype(BF16), pose.astype(F32), b1e.astype(F32), w2e.astype(BF16)


def _position_tables(seq):
    t = np.arange(seq)
    hi = (t // SEL_BLOCK) * SEL_BLOCK
    lo = t % SEL_BLOCK
    qtab = np.zeros((seq, LANES), np.float32)
    qtab[:, FEAT + 0] = -hi
    qtab[:, FEAT + 1] = -lo
    qtab[:, FEAT + 2] = 1.0
    qtab[:, FEAT + 3] = 1.0
    ktab = np.zeros((seq, LANES), np.float32)
    ktab[:, FEAT + 0] = 1.0
    ktab[:, FEAT + 1] = 1.0
    ktab[:, FEAT + 2] = hi
    ktab[:, FEAT + 3] = lo
    ktab[t, SEL_LANE0 + t // SEL_BLOCK] = 1.0
    per = SEL_BLOCK // CMP_STRIDE
    n_sb = seq // SEL_BLOCK
    r = np.arange(per * n_sb)
    end = (per * (r % n_sb) + r // n_sb) * CMP_STRIDE + CMP_BLOCK - 1
    ctab = np.zeros((per * n_sb, NSA_KV_GROUPS, LANES), np.float32)
    ctab[:, :, FEAT + 0] = 1.0
    ctab[:, :, FEAT + 1] = 1.0
    ctab[:, :, FEAT + 2] = ((end // SEL_BLOCK) * SEL_BLOCK)[:, None]
    ctab[:, :, FEAT + 3] = (end % SEL_BLOCK)[:, None]
    return jnp.asarray(qtab), jnp.asarray(ktab), jnp.asarray(ctab.reshape(per * n_sb, NSA_KV_GROUPS * LANES))


def _rope_table(positions):
    inv_freq = ROPE_THETA ** (-jnp.arange(0, QK_ROPE, 2, dtype=F32) / QK_ROPE)
    ang = positions.astype(F32)[..., None] * inv_freq
    cos, sin = jnp.cos(ang), jnp.sin(ang)
    c2 = jnp.concatenate([cos, cos], axis=-1)
    s2 = jnp.concatenate([-sin, sin], axis=-1)
    scale = (QK_NOPE + QK_ROPE) ** -0.5
    lead = c2.shape[:-1]
    mq = jnp.concatenate([jnp.full(lead + (QK_NOPE,), scale, F32), scale * c2, scale * s2], axis=-1)
    rk = jnp.concatenate([jnp.zeros(lead + (_SMALL_KR,), F32), c2, s2,
                          jnp.zeros(lead + (LANES - _SMALL_KR - 2 * QK_ROPE,), F32)], axis=-1)
    return jnp.concatenate([mq, rk], axis=-1).reshape(-1, 2 * LANES)


def _rope_placement():
    place = np.zeros((LANES, MLA_HEADS, LANES), np.float32)
    i = np.arange(QK_ROPE)
    place[_SMALL_KR + i, :, QK_NOPE + i] = 1.0
    place[_SMALL_KR + i, :, QK_NOPE + QK_ROPE + i] = 1.0
    return jnp.asarray(place.reshape(LANES, MLA_HEADS * LANES), dtype=BF16)


def kernel(x, p, positions, g_pre_mix, w_in, nsa_pos_k, nsa_pos_v, nsa_ck_w1, nsa_ck_b1, nsa_ck_w2, nsa_cv_w1, nsa_cv_b1, nsa_cv_w2, mla_g_q, mla_w_uq, mla_g_kv, mla_w_ukv, w_br_nsa, w_br_mla, w_o, g_post_mix, g_pre_ffn, w_up, w_conv, b_conv, w_down, g_post_ffn, w_ple, w_ple_gate, g_ple):
    b, t, d = x.shape
    depth = w_in.shape[0]
    assert d == D_MODEL and t // SEL_BLOCK == LANES - SEL_LANE0 and t % PROJ_ROWS == 0
    n = b * t
    qtab, ktab, ctab = _position_tables(t)
    ropetab = _rope_table(positions)
    place = _rope_placement()
    xc = x.reshape(n, d)
    for i in range(depth):
        w_all = _layout_w_in(w_in[i])
        wuq = mla_w_uq[i].reshape(Q_LORA, MLA_HEADS, QK_NOPE + QK_ROPE)
        wuq = jnp.concatenate([wuq, _swap_halves(wuq[..., QK_NOPE:])], axis=-1).reshape(Q_LORA, MLA_HEADS * LANES)
        wukv = mla_w_ukv[i].reshape(KV_LORA, MLA_HEADS, QK_NOPE + V_DIM)
        wk = _pad_heads(wukv[..., :QK_NOPE].reshape(KV_LORA, -1), MLA_HEADS, QK_NOPE)
        wv = _pad_heads(wukv[..., QK_NOPE:].reshape(KV_LORA, -1), MLA_HEADS, V_DIM)
        (q_n, kc, vc, ks, vs, kw, vw, gs, q_m, k_m, v_m, gm) = _proj_call(
            xc, g_pre_mix[i][None], w_all, wuq.astype(BF16), mla_g_q[i][None], wk.astype(BF16), wv.astype(BF16),
            mla_g_kv[i][None], place, qtab, ktab, ropetab, t)

        n_sb = t // SEL_BLOCK
        kcmp, vcmp = _compress_call(
            kc.reshape(b, n_sb, SEL_BLOCK * LANES), vc.reshape(b, n_sb, SEL_BLOCK * LANES),
            *_layout_compress(nsa_ck_w1[i], nsa_pos_k[i], nsa_ck_b1[i], nsa_ck_w2[i]),
            *_layout_compress(nsa_cv_w1[i], nsa_pos_v[i], nsa_cv_b1[i], nsa_cv_w2[i]), ctab)

        def seq3(a):
            return a.reshape(b, t, a.shape[-1])

        o_nsa = _nsa_call(seq3(q_n), kcmp, vcmp, seq3(ks), seq3(vs), seq3(kw), seq3(vw), seq3(gs))
        o_mla = _mla_call(q_m, k_m, v_m)

        x1, h2 = _merge_call(o_nsa.reshape(n, NSA_WIDTH), o_mla, gm, xc, w_br_nsa[i].astype(BF16),
                             w_br_mla[i].astype(BF16).reshape(-1, LANES, D_MODEL), w_o[i].astype(BF16),
                             g_post_mix[i][None], g_pre_ffn[i][None])
        x2 = _ffn_call(h2, x1, w_up[i].astype(BF16), w_conv[i], b_conv[i][None], w_down[i].astype(BF16),
                       g_post_ffn[i][None], t)
        xc = _ple_call(p[i].reshape(n, PLE_DIM), x2, w_ple[i].astype(BF16), w_ple_gate[i].astype(BF16),
                       g_ple[i][None])
    return xc.reshape(b, t, d)
```

```python
import functools
import math

import numpy as np
import jax
import jax.numpy as jnp
from jax import lax
from jax.experimental import pallas as pl
from jax.experimental.pallas import tpu as pltpu

F32 = jnp.float32
BF16 = jnp.bfloat16

D_MODEL = 1024
PLE_DIM = 256
NSA_HEADS = 8
NSA_KV_GROUPS = 2
NSA_REP = NSA_HEADS // NSA_KV_GROUPS
NSA_HEAD_DIM = 64
NSA_WIDTH = NSA_HEADS * NSA_HEAD_DIM
CMP_BLOCK = 32
CMP_STRIDE = 16
SEL_BLOCK = 64
N_SEL = 16
WINDOW = 512
MLA_HEADS = 8
Q_LORA = 256
KV_LORA = 128
QK_NOPE = 64
QK_ROPE = 32
V_DIM = 64
MLA_WIDTH = MLA_HEADS * V_DIM
ROPE_THETA = 10000.0
D_FF = 2816
CONV_WIDTH = 3
EPS = 1e-6
NEG = -1e30
BIG = 1e9
ALIBI_SLOPES = tuple(2.0 ** (-8.0 * (i + 1) / NSA_HEADS) for i in range(NSA_HEADS))

LANES = 128
SUBLANES = 8

FEAT = NSA_HEAD_DIM
SEL_LANE0 = 96
ONES_LANE = NSA_HEAD_DIM

PROJ_ROWS = 512
NSA_TQ = 128
NSA_SEL_SLAB = 512
MLA_TQ = 256
MLA_SLAB = 512
MERGE_ROWS = 512
FFN_ROWS = 512
FFN_CHUNK = 256
FFN_GROUP = 4

_C_Q = 0
_C_KC = _C_Q + NSA_WIDTH
_C_VC = _C_KC + LANES
_C_KS = _C_VC + LANES
_C_VS = _C_KS + LANES
_C_KW = _C_VS + LANES
_C_VW = _C_KW + LANES
_C_SMALL = _C_VW + LANES
_C_CQ = _C_SMALL + LANES
_C_CKV = _C_CQ + Q_LORA
_C_MERGE = _C_CKV + KV_LORA
_C_END = _C_MERGE + 2 * D_MODEL
_SMALL_KR = 32


def _vmem_params(semantics, mib):
    return pltpu.CompilerParams(dimension_semantics=semantics, vmem_limit_bytes=mib * 1024 * 1024)


def _const_spec(shape):
    nd = len(shape)
    return pl.BlockSpec(shape, lambda *_: (0,) * nd, pipeline_mode=pl.Buffered(1))


def _rms(x, g):
    return x * lax.rsqrt(jnp.mean(x * x, axis=-1, keepdims=True) + EPS) * g


def _gelu_tanh(x):
    return 0.5 * x * (1.0 + jnp.tanh(math.sqrt(2.0 / math.pi) * (x + 0.044715 * (x * x * x))))


def _dot(a, b):
    return jnp.dot(a, b, preferred_element_type=F32)


def _dot_nt(a, b):
    return lax.dot_general(a, b, (((1,), (1,)), ((), ())), preferred_element_type=F32)


def _dot_tn(a, b):
    return lax.dot_general(a, b, (((0,), (0,)), ((), ())), preferred_element_type=F32)


def _proj_kernel(x_ref, g_ref, w_ref, wuq_ref, gq_ref, wk_ref, wv_ref, gkv_ref, place_ref, qtab_ref, ktab_ref,
                 rope_ref, q_out, kc_out, vc_out, ks_out, vs_out, kw_out, vw_out, gs_out, qm_out, km_out,
                 vm_out, gm_out):
    rows = x_ref.shape[0]
    h = _rms(x_ref[...], g_ref[...]).astype(BF16)

    def proj(a, b):
        return _dot(h, w_ref[:, a:b])

    lane = lax.broadcasted_iota(jnp.int32, (rows, LANES), 1)
    ones_col = (lane == ONES_LANE).astype(F32)

    data = lane < NSA_HEAD_DIM

    def halves(col):
        packed = proj(col, col + LANES)
        return packed, pltpu.roll(packed, LANES - NSA_HEAD_DIM, 1)

    qtab = qtab_ref[...]
    for pr in range(NSA_HEADS // 2):
        for hd, qh in zip((2 * pr, 2 * pr + 1), halves(_C_Q + pr * LANES)):
            q_out[:, hd * LANES:(hd + 1) * LANES] = jnp.where(data, qh, ALIBI_SLOPES[hd] * qtab).astype(BF16)

    kc_out[...] = proj(_C_KC, _C_KC + LANES).astype(BF16)
    vc_out[...] = proj(_C_VC, _C_VC + LANES).astype(BF16)

    ktab_sel = ktab_ref[...]
    ktab_win = jnp.where(lane < SEL_LANE0, ktab_sel, 0.0)
    for col, out, feat in ((_C_KS, ks_out, ktab_sel), (_C_VS, vs_out, ones_col), (_C_KW, kw_out, ktab_win),
                           (_C_VW, vw_out, ones_col)):
        for g, part in enumerate(halves(col)):
            out[:, g * LANES:(g + 1) * LANES] = jnp.where(data, part, feat).astype(BF16)

    small = proj(_C_SMALL, _C_SMALL + LANES)
    gs_out[...] = jax.nn.sigmoid(small).astype(BF16)

    mq = rope_ref[:, 0:LANES]
    rk = rope_ref[:, LANES:2 * LANES]
    t = small * rk
    k_rope = (t + pltpu.roll(t, LANES - QK_ROPE, 1)).astype(BF16)

    cqn = _rms(proj(_C_CQ, _C_CQ + Q_LORA), gq_ref[...]).astype(BF16)
    for hd in range(MLA_HEADS):
        sl = slice(hd * LANES, (hd + 1) * LANES)
        qm_out[0, hd] = (_dot(cqn, wuq_ref[:, sl]) * mq).astype(BF16)

    ckvn = _rms(proj(_C_CKV, _C_CKV + KV_LORA), gkv_ref[...]).astype(BF16)
    for hd in range(MLA_HEADS):
        sl = slice(hd * LANES, (hd + 1) * LANES)
        km_out[0, hd] = (_dot(ckvn, wk_ref[:, sl]) + _dot(k_rope, place_ref[:, sl])).astype(BF16)
        vm_out[0, hd] = (_dot(ckvn, wv_ref[:, sl]) + ones_col).astype(BF16)

    half = D_MODEL // 2
    for c in range(2 * D_MODEL // half):
        a = _C_MERGE + c * half
        gm_out[:, c * half:(c + 1) * half] = jax.nn.sigmoid(proj(a, a + half)).astype(BF16)


def _proj_call(x2d, g_pre, w_all, wuq, g_q, wk, wv, g_kv, place, qtab, ktab, ropetab, seq):
    n = x2d.shape[0]
    rows = PROJ_ROWS
    per_seq = seq // rows

    def row_spec(width):
        return pl.BlockSpec((rows, width), lambda i: (i, 0))

    tab_spec = pl.BlockSpec((rows, LANES), lambda i: (i % per_seq, 0))
    head_spec = pl.BlockSpec((1, MLA_HEADS, rows, LANES), lambda i: (i // per_seq, 0, i % per_seq, 0))
    head_shape = jax.ShapeDtypeStruct((n // seq, MLA_HEADS, seq, LANES), BF16)
    widths = (NSA_HEADS * LANES, LANES, LANES, NSA_KV_GROUPS * LANES, NSA_KV_GROUPS * LANES,
              NSA_KV_GROUPS * LANES, NSA_KV_GROUPS * LANES, LANES)
    return pl.pallas_call(
        _proj_kernel,
        grid=(n // rows,),
        in_specs=[row_spec(D_MODEL), _const_spec(g_pre.shape), _const_spec(w_all.shape), _const_spec(wuq.shape),
                  _const_spec(g_q.shape), _const_spec(wk.shape), _const_spec(wv.shape), _const_spec(g_kv.shape),
                  _const_spec(place.shape), tab_spec, tab_spec, row_spec(2 * LANES)],
        out_specs=[row_spec(w) for w in widths] + [head_spec] * 3 + [row_spec(2 * D_MODEL)],
        out_shape=[jax.ShapeDtypeStruct((n, w), BF16) for w in widths] + [head_shape] * 3
        + [jax.ShapeDtypeStruct((n, 2 * D_MODEL), BF16)],
        compiler_params=_vmem_params(("parallel",), 48),
        name="proj",
    )(x2d, g_pre, w_all, wuq, g_q, wk, wv, g_kv, place, qtab, ktab, ropetab)


def _compress_kernel(kx_ref, vx_ref, w1k_ref, posk_ref, b1k_ref, w2k_ref, w1v_ref, posv_ref, b1v_ref, w2v_ref,
                     ktab_ref, kc_out, vc_out):
    n_sb = kx_ref.shape[1]
    half = w1k_ref.shape[0] // 2
    per = SEL_BLOCK // CMP_STRIDE

    def one(x_ref, w1_ref, pos_ref, b1_ref, w2_ref):
        x = x_ref[0].astype(F32)
        tops, bots = [], []
        for c in range(per):
            xc = x[:, c * half:(c + 1) * half]
            tops.append(_dot((xc + pos_ref[:, 0:half]).astype(BF16), w1_ref[0:half, :]))
            bots.append(_dot((xc + pos_ref[:, half:2 * half]).astype(BF16), w1_ref[half:2 * half, :]))
        nxt = pltpu.roll(bots[0], n_sb - 1, 0)
        row = lax.broadcasted_iota(jnp.int32, nxt.shape, 0)
        nxt = jnp.where(row == n_sb - 1, 0.0, nxt)
        pre = jnp.concatenate([tops[c] + (bots[c + 1] if c + 1 < per else nxt) for c in range(per)], axis=0)
        return _dot(_gelu_tanh(pre + b1_ref[...]).astype(BF16), w2_ref[...])

    kc_out[0] = (one(kx_ref, w1k_ref, posk_ref, b1k_ref, w2k_ref) + ktab_ref[...]).astype(BF16)
    vc_out[0] = one(vx_ref, w1v_ref, posv_ref, b1v_ref, w2v_ref).astype(BF16)


def _compress_call(kx, vx, w1k, posk, b1k, w2k, w1v, posv, b1v, w2v, ktab):
    b, n_sb, width = kx.shape
    n_rows = n_sb * (SEL_BLOCK // CMP_STRIDE)
    x_spec = pl.BlockSpec((1, n_sb, width), lambda i: (i, 0, 0))
    o_spec = pl.BlockSpec((1, n_rows, NSA_KV_GROUPS * LANES), lambda i: (i, 0, 0))
    consts = (w1k, posk, b1k, w2k, w1v, posv, b1v, w2v, ktab)
    return pl.pallas_call(
        _compress_kernel,
        grid=(b,),
        in_specs=[x_spec, x_spec] + [_const_spec(c.shape) for c in consts],
        out_specs=[o_spec, o_spec],
        out_shape=[jax.ShapeDtypeStruct((b, n_rows, NSA_KV_GROUPS * LANES), BF16)] * 2,
        compiler_params=_vmem_params(("parallel",), 32),
        name="compress",
    )(kx, vx, *consts)


def _normalize(acc):
    return acc * (1.0 / acc[:, ONES_LANE:ONES_LANE + 1])


def _dense_attend(q, tiles, k_at, v_at, s_ref):
    maxes = []
    off = 0
    for start, width, mask in tiles:
        s = _dot_nt(q, k_at(start, width))
        if mask is not None:
            s = jnp.where(mask, s, NEG)
        s_ref[:, off:off + width] = s
        maxes.append(jnp.max(s, axis=1, keepdims=True))
        off += width
    m = functools.reduce(jnp.maximum, maxes)
    acc = None
    off = 0
    for start, width, _ in tiles:
        part = _dot(jnp.exp(s_ref[:, off:off + width] - m).astype(BF16), v_at(start, width))
        acc = part if acc is None else acc + part
        off += width
    return _normalize(acc)


def _split_tiles(start, total, width):
    return [(start + o, min(width, total - o), None) for o in range(0, total, width)]


def _nsa_kernel(q_ref, kcmp_ref, vcmp_ref, ks_ref, vs_ref, kw_ref, vw_ref, gs_ref, o_ref, qsel_ref, ocmp_ref,
                osel_ref, owin_ref, s_ref):
    tq = q_ref.shape[1]
    seq = ks_ref.shape[1]
    qi = pl.program_id(1)
    t0 = qi * tq
    m_rows = NSA_REP * tq
    n_cmp_rows = kcmp_ref.shape[1]
    n_sb = n_cmp_rows // (SEL_BLOCK // CMP_STRIDE)
    per = SEL_BLOCK // CMP_STRIDE
    win_tiles = WINDOW // tq
    groups = [slice(g * LANES, (g + 1) * LANES) for g in range(NSA_KV_GROUPS)]

    def q_group(g):
        return jnp.concatenate(
            [q_ref[0, :, (g * NSA_REP + r) * LANES:(g * NSA_REP + r + 1) * LANES] for r in range(NSA_REP)], axis=0)

    def rel(width):
        return (lax.broadcasted_iota(jnp.int32, (m_rows, width), 1)
                - lax.broadcasted_iota(jnp.int32, (m_rows, width), 0) % tq)

    lane_q = lax.broadcasted_iota(jnp.int32, (m_rows, LANES), 1)
    for g, gl in enumerate(groups):
        qg = q_group(g)

        s_t = _dot_nt(kcmp_ref[0, :, gl], qg)
        krow = lax.broadcasted_iota(jnp.int32, s_t.shape, 0)
        blk_end = (per * (krow % n_sb) + krow // n_sb) * CMP_STRIDE + (CMP_BLOCK - 1)
        t_abs = t0 + lax.broadcasted_iota(jnp.int32, s_t.shape, 1) % tq
        cmask = t_abs >= blk_end
        s_t = jnp.where(cmask, s_t, NEG)
        e = jnp.exp(s_t - jnp.max(s_t, axis=0, keepdims=True))
        p_t = jnp.where(cmask, e, 0.0) * (1.0 / jnp.sum(e, axis=0, keepdims=True))
        o_cmp = _dot_tn(p_t.astype(BF16), vcmp_ref[0, :, gl])

        imp = None
        for r in range(NSA_REP):
            for c in range(per):
                piece = p_t[c * n_sb:(c + 1) * n_sb, r * tq:(r + 1) * tq]
                imp = piece if imp is None else imp + piece
        sb = lax.broadcasted_iota(jnp.int32, (n_sb, tq), 0)
        cur = (t0 + lax.broadcasted_iota(jnp.int32, (n_sb, tq), 1)) // SEL_BLOCK
        forced = (sb == 0) | (sb == cur) | (sb == cur - 1)
        score = jnp.where(forced, BIG, jnp.where(sb > cur, -BIG, imp))
        rank = jnp.zeros((n_sb, tq), jnp.int32)
        for j in range(n_sb):
            sj = score[j:j + 1, :]
            before = (sj > score) | ((sj == score) & (sb > j))
            rank = rank + before.astype(jnp.int32)
        sel_neg = jnp.where(rank < min(N_SEL, n_sb), 0.0, NEG)
        sel_q = jnp.concatenate([jnp.zeros((SEL_LANE0, tq), F32), sel_neg], axis=0).T.astype(BF16)
        qsel_ref[g] = jnp.where(lane_q >= SEL_LANE0, jnp.concatenate([sel_q] * NSA_REP, axis=0), qg)
        ocmp_ref[g] = o_cmp

    def kv_at(k_ref, gl):
        return lambda start, width: k_ref[0, pl.ds(start, width), gl]

    slab = NSA_SEL_SLAB
    for n in range(seq // slab):

        @pl.when(qi // (slab // tq) == n)
        def _(n=n):
            last = rel(slab) <= t0 - n * slab
            tiles = _split_tiles(0, n * slab, slab) + [(n * slab, slab, last)]
            for g, gl in enumerate(groups):
                osel_ref[g] = _dense_attend(qsel_ref[g], tiles, kv_at(ks_ref, gl), kv_at(vs_ref, gl), s_ref.at[g])

    for v in range(win_tiles + 1):

        @pl.when((qi == v) if v < win_tiles else (qi >= win_tiles))
        def _(v=v):
            causal = rel(tq) <= 0
            if v < win_tiles:
                tiles = _split_tiles(0, v * tq, WINDOW) + [(v * tq, tq, causal)]
            else:
                first = pl.multiple_of((qi - win_tiles) * tq, tq)
                tiles = [(first, tq, rel(tq) > 0), (pl.multiple_of(first + tq, tq), WINDOW - tq, None),
                         (pl.multiple_of(t0, tq), tq, causal)]
            for g, gl in enumerate(groups):
                owin_ref[g] = _dense_attend(q_group(g), tiles, kv_at(kw_ref, gl), kv_at(vw_ref, gl), s_ref.at[g])

    gates = gs_ref[0].astype(F32)
    outs = []
    for g in range(NSA_KV_GROUPS):
        for r in range(NSA_REP):
            hd = g * NSA_REP + r
            rs = slice(r * tq, (r + 1) * tq)
            o = (gates[:, hd:hd + 1] * ocmp_ref[g, rs, :]
                 + gates[:, NSA_HEADS + hd:NSA_HEADS + hd + 1] * osel_ref[g, rs, :]
                 + gates[:, 2 * NSA_HEADS + hd:2 * NSA_HEADS + hd + 1] * owin_ref[g, rs, :])
            outs.append(o[:, 0:NSA_HEAD_DIM])
    o_ref[0] = jnp.concatenate(outs, axis=1).astype(o_ref.dtype)


def _nsa_call(q, kcmp, vcmp, ks, vs, kw, vw, gs):
    b, t, _ = q.shape
    tq = NSA_TQ
    m_rows = NSA_REP * tq
    branch_out = pltpu.VMEM((NSA_KV_GROUPS, m_rows, LANES), F32)
    kv_spec = pl.BlockSpec((1, t, NSA_KV_GROUPS * LANES), lambda i, j: (i, 0, 0))
    cmp_spec = pl.BlockSpec((1, kcmp.shape[1], NSA_KV_GROUPS * LANES), lambda i, j: (i, 0, 0))
    return pl.pallas_call(
        _nsa_kernel,
        grid=(b, t // tq),
        in_specs=[pl.BlockSpec((1, tq, NSA_HEADS * LANES), lambda i, j: (i, j, 0)), cmp_spec, cmp_spec,
                  kv_spec, kv_spec, kv_spec, kv_spec, pl.BlockSpec((1, tq, LANES), lambda i, j: (i, j, 0))],
        out_specs=pl.BlockSpec((1, tq, NSA_WIDTH), lambda i, j: (i, j, 0)),
        out_shape=jax.ShapeDtypeStruct((b, t, NSA_WIDTH), BF16),
        scratch_shapes=[pltpu.VMEM((NSA_KV_GROUPS, m_rows, LANES), BF16), branch_out, branch_out, branch_out,
                        pltpu.VMEM((NSA_KV_GROUPS, m_rows, t), F32)],
        compiler_params=_vmem_params(("parallel", "arbitrary"), 48),
        name="nsa",
    )(q, kcmp, vcmp, ks, vs, kw, vw, gs)


def _mla_kernel(qlo_ref, qhi_ref, k_ref, v_ref, o_ref, s_ref):
    tq = qlo_ref.shape[2]
    nq = k_ref.shape[2] // tq
    step = pl.program_id(1)
    pairs = o_ref.shape[1]
    per_pair = MLA_HEADS // pairs

    for n in range(nq // 2):

        @pl.when(step == n)
        def _(n=n):
            causal = (lax.broadcasted_iota(jnp.int32, (tq, tq), 1) <= lax.broadcasted_iota(jnp.int32, (tq, tq), 0))

            def pair_body(pr, carry):
                for which, (q_ref, qt) in enumerate(((qlo_ref, n), (qhi_ref, nq - 1 - n))):
                    tiles = _split_tiles(0, qt * tq, MLA_SLAB) + [(qt * tq, tq, causal)]
                    outs = []
                    for e in range(per_pair):
                        hd = per_pair * pr + e
                        o = _dense_attend(
                            q_ref[0, hd], tiles,
                            lambda start, width, hd=hd: k_ref[0, hd, pl.ds(start, width), :],
                            lambda start, width, hd=hd: v_ref[0, hd, pl.ds(start, width), :],
                            s_ref.at[which * per_pair + e])
                        outs.append(o[:, 0:V_DIM])
                    o_ref[0, pr, qt * tq:(qt + 1) * tq, :] = jnp.concatenate(outs, axis=1).astype(o_ref.dtype)
                return carry

            lax.fori_loop(0, pairs, pair_body, 0)


def _mla_call(q, k, v):
    b, heads, t, width = q.shape
    tq = MLA_TQ
    nq = t // tq
    pairs = heads * V_DIM // LANES
    kv_spec = pl.BlockSpec((1, heads, t, width), lambda i, j: (i, 0, 0, 0))
    return pl.pallas_call(
        _mla_kernel,
        grid=(b, nq // 2),
        in_specs=[pl.BlockSpec((1, heads, tq, width), lambda i, j: (i, 0, j, 0)),
                  pl.BlockSpec((1, heads, tq, width), lambda i, j: (i, 0, nq - 1 - j, 0)), kv_spec, kv_spec],
        out_specs=pl.BlockSpec((1, pairs, t, LANES), lambda i, j: (i, 0, 0, 0)),
        out_shape=jax.ShapeDtypeStruct((b, pairs, t, LANES), BF16),
        scratch_shapes=[pltpu.VMEM((2 * (heads // pairs), tq, t), F32)],
        compiler_params=_vmem_params(("parallel", "arbitrary"), 56),
        name="mla",
    )(q, q, k, v)


def _merge_kernel(on_ref, om_ref, gm_ref, x_ref, wbn_ref, wbm_ref, wo_ref, gpost_ref, gpre_ref, x1_out, h_out):
    a = _dot(on_ref[...], wbn_ref[...])
    b = _dot(om_ref[0, 0], wbm_ref[0])
    for pr in range(1, om_ref.shape[1]):
        b = b + _dot(om_ref[0, pr], wbm_ref[pr])
    merged = gm_ref[:, 0:D_MODEL].astype(F32) * a + gm_ref[:, D_MODEL:2 * D_MODEL].astype(F32) * b
    y = _dot(merged.astype(BF16), wo_ref[...])
    x1 = x_ref[...] + _rms(y, gpost_ref[...])
    x1_out[...] = x1
    h_out[...] = _rms(x1, gpre_ref[...]).astype(BF16)


def _merge_call(o_nsa, o_mla, gm, x2d, wbn, wbm, wo, g_post, g_pre_ffn):
    n = x2d.shape[0]
    rows = MERGE_ROWS
    _, pairs, seq, _ = o_mla.shape
    per_seq = seq // rows

    def row_spec(width):
        return pl.BlockSpec((rows, width), lambda i: (i, 0))

    mla_spec = pl.BlockSpec((1, pairs, rows, LANES), lambda i: (i // per_seq, 0, i % per_seq, 0))
    consts = (wbn, wbm, wo, g_post, g_pre_ffn)
    return pl.pallas_call(
        _merge_kernel,
        grid=(n // rows,),
        in_specs=[row_spec(NSA_WIDTH), mla_spec, row_spec(2 * D_MODEL), row_spec(D_MODEL)]
        + [_const_spec(c.shape) for c in consts],
        out_specs=[row_spec(D_MODEL), row_spec(D_MODEL)],
        out_shape=[jax.ShapeDtypeStruct((n, D_MODEL), F32), jax.ShapeDtypeStruct((n, D_MODEL), BF16)],
        compiler_params=_vmem_params(("parallel",), 40),
        name="merge",
    )(o_nsa, o_mla, gm, x2d, *consts)


def _ffn_kernel(h_ref, x1_ref, p_ref, wup_ref, wconv_ref, bconv_ref, wdown_ref, g_ref, wple_ref, wpg_ref, gple_ref,
                out_ref, halo_ref, sa_ref, sv_ref, gate_ref, *, tiles_per_seq):
    rows = h_ref.shape[0]
    pad = SUBLANES
    keep = CONV_WIDTH - 1
    seq_start = pl.program_id(0) % tiles_per_seq == 0
    h = h_ref[...]

    @pl.when(pl.program_id(0) == 0)
    def _():
        halo_ref[...] = jnp.zeros_like(halo_ref)

    def conv_chunk(off, s_ref):
        cols = slice(off, off + FFN_CHUNK)
        u = _dot(h, wup_ref[:, cols])
        prev = jnp.where(seq_start, 0.0, halo_ref[pad - keep:pad, cols])
        s_ref[pad - keep:pad, :] = prev
        s_ref[pad:pad + rows, :] = u
        halo_ref[pad - keep:pad, cols] = s_ref[pad + rows - keep:pad + rows, :]
        y = wconv_ref[CONV_WIDTH - 1:CONV_WIDTH, cols] * u + bconv_ref[:, cols]
        for k in range(CONV_WIDTH - 1):
            shift = CONV_WIDTH - 1 - k
            y = y + wconv_ref[k:k + 1, cols] * s_ref[pad - shift:pad - shift + rows, :]
        return y

    n_chunks = D_FF // FFN_CHUNK
    acc = None
    for c0 in range(0, n_chunks, FFN_GROUP):
        chunks = range(c0, min(c0 + FFN_GROUP, n_chunks))
        slot = (c0 // FFN_GROUP) % 2
        for j, c in enumerate(chunks):
            a = conv_chunk(c * FFN_CHUNK, sa_ref.at[c % 2])
            v = conv_chunk(D_FF + c * FFN_CHUNK, sv_ref.at[c % 2])
            gate_ref[slot, :, j * FFN_CHUNK:(j + 1) * FFN_CHUNK] = (_gelu_tanh(a) * v).astype(BF16)
        width = len(chunks) * FFN_CHUNK
        part = _dot(gate_ref[slot, :, 0:width], wdown_ref[c0 * FFN_CHUNK:c0 * FFN_CHUNK + width, :])
        acc = part if acc is None else acc + part
    x2 = x1_ref[...] + _rms(acc, g_ref[...])
    e = _dot(p_ref[...].astype(BF16), wple_ref[...])
    gate = jax.nn.sigmoid(_dot(x2.astype(BF16), wpg_ref[...]))
    out_ref[...] = x2 + _rms(e * gate, gple_ref[...])


def _ffn_call(h2, x1, p2d, wup, wconv, bconv, wdown, g_post_ffn, wple, wgate, g_ple, seq):
    n = x1.shape[0]
    rows = FFN_ROWS

    def row_spec(width=D_MODEL):
        return pl.BlockSpec((rows, width), lambda i: (i, 0))

    consts = (wup, wconv, bconv, wdown, g_post_ffn, wple, wgate, g_ple)
    return pl.pallas_call(
        functools.partial(_ffn_kernel, tiles_per_seq=seq // rows),
        grid=(n // rows,),
        in_specs=[row_spec(), row_spec(), row_spec(PLE_DIM)] + [_const_spec(c.shape) for c in consts],
        out_specs=row_spec(),
        out_shape=jax.ShapeDtypeStruct((n, D_MODEL), F32),
        scratch_shapes=[pltpu.VMEM((SUBLANES, 2 * D_FF), F32),
                        pltpu.VMEM((2, rows + SUBLANES, FFN_CHUNK), F32),
                        pltpu.VMEM((2, rows + SUBLANES, FFN_CHUNK), F32),
                        pltpu.VMEM((2, rows, FFN_GROUP * FFN_CHUNK), BF16)],
        compiler_params=_vmem_params(("arbitrary",), 48),
        name="ffn",
    )(h2, x1, p2d, *consts)


def _pad_heads(w, heads, width):
    rows = w.shape[0]
    w = w.reshape(rows, heads, width)
    return jnp.pad(w, ((0, 0), (0, 0), (0, LANES - width))).reshape(rows, heads * LANES)


def _swap_halves(w):
    half = w.shape[-1] // 2
    return jnp.concatenate([w[..., half:], w[..., :half]], axis=-1)


def _layout_w_in(w_in):
    sizes = (NSA_WIDTH, 6 * NSA_KV_GROUPS * NSA_HEAD_DIM, 3 * NSA_HEADS, Q_LORA, KV_LORA, QK_ROPE, 2 * D_MODEL)
    offs = np.cumsum((0,) + sizes)
    wq, wkv, wgate, wcq, wckv, wkr, wmerge = (w_in[:, offs[i]:offs[i + 1]] for i in range(len(sizes)))
    rows = w_in.shape[0]
    small = jnp.concatenate([wgate, jnp.zeros((rows, _SMALL_KR - wgate.shape[1]), w_in.dtype), wkr, _swap_halves(wkr),
                             jnp.zeros((rows, LANES - _SMALL_KR - 2 * QK_ROPE), w_in.dtype)], axis=1)
    parts = [wq * (NSA_HEAD_DIM ** -0.5), wkv, small, wcq, wckv, wmerge]
    w_all = jnp.concatenate([part.astype(BF16) for part in parts], axis=1)
    assert w_all.shape[1] == _C_END
    return w_all


def _layout_compress(w1, pos, b1, w2):
    dh, g, half_tok = NSA_HEAD_DIM, NSA_KV_GROUPS, CMP_BLOCK // 2
    eye = jnp.eye(g, dtype=w1.dtype)
    w1 = w1.reshape(CMP_BLOCK, dh, dh)
    w1e = jnp.einsum("lde,gh->lgdhe", w1, eye).reshape(CMP_BLOCK * g * dh, g * dh)
    pose = jnp.broadcast_to(pos[:, None, :], (CMP_BLOCK, g, dh)).reshape(1, CMP_BLOCK * g * dh)
    b1e = jnp.tile(b1, g).reshape(1, g * dh)
    w2e = jnp.einsum("de,gh->gdhe", w2, eye)
    w2e = jnp.pad(w2e, ((0, 0), (0, 0), (0, 0), (0, LANES - dh))).reshape(g * dh, g * LANES)
    del half_tok
    return w1e.astype(BF16), pose.astype(F32), b1e.astype(F32), w2e.astype(BF16)


def _position_tables(seq):
    t = np.arange(seq)
    hi = (t // SEL_BLOCK) * SEL_BLOCK
    lo = t % SEL_BLOCK
    qtab = np.zeros((seq, LANES), np.float32)
    qtab[:, FEAT + 0] = -hi
    qtab[:, FEAT + 1] = -lo
    qtab[:, FEAT + 2] = 1.0
    qtab[:, FEAT + 3] = 1.0
    ktab = np.zeros((seq, LANES), np.float32)
    ktab[:, FEAT + 0] = 1.0
    ktab[:, FEAT + 1] = 1.0
    ktab[:, FEAT + 2] = hi
    ktab[:, FEAT + 3] = lo
    ktab[t, SEL_LANE0 + t // SEL_BLOCK] = 1.0
    per = SEL_BLOCK // CMP_STRIDE
    n_sb = seq // SEL_BLOCK
    r = np.arange(per * n_sb)
    end = (per * (r % n_sb) + r // n_sb) * CMP_STRIDE + CMP_BLOCK - 1
    ctab = np.zeros((per * n_sb, NSA_KV_GROUPS, LANES), np.float32)
    ctab[:, :, FEAT + 0] = 1.0
    ctab[:, :, FEAT + 1] = 1.0
    ctab[:, :, FEAT + 2] = ((end // SEL_BLOCK) * SEL_BLOCK)[:, None]
    ctab[:, :, FEAT + 3] = (end % SEL_BLOCK)[:, None]
    return jnp.asarray(qtab), jnp.asarray(ktab), jnp.asarray(ctab.reshape(per * n_sb, NSA_KV_GROUPS * LANES))


def _rope_table(positions):
    inv_freq = ROPE_THETA ** (-jnp.arange(0, QK_ROPE, 2, dtype=F32) / QK_ROPE)
    ang = positions.astype(F32)[..., None] * inv_freq
    cos, sin = jnp.cos(ang), jnp.sin(ang)
    c2 = jnp.concatenate([cos, cos], axis=-1)
    s2 = jnp.concatenate([-sin, sin], axis=-1)
    scale = (QK_NOPE + QK_ROPE) ** -0.5
    lead = c2.shape[:-1]
    mq = jnp.concatenate([jnp.full(lead + (QK_NOPE,), scale, F32), scale * c2, scale * s2], axis=-1)
    rk = jnp.concatenate([jnp.zeros(lead + (_SMALL_KR,), F32), c2, s2,
                          jnp.zeros(lead + (LANES - _SMALL_KR - 2 * QK_ROPE,), F32)], axis=-1)
    return jnp.concatenate([mq, rk], axis=-1).reshape(-1, 2 * LANES)


def _rope_placement():
    place = np.zeros((LANES, MLA_HEADS, LANES), np.float32)
    i = np.arange(QK_ROPE)
    place[_SMALL_KR + i, :, QK_NOPE + i] = 1.0
    place[_SMALL_KR + i, :, QK_NOPE + QK_ROPE + i] = 1.0
    return jnp.asarray(place.reshape(LANES, MLA_HEADS * LANES), dtype=BF16)


def kernel(x, p, positions, g_pre_mix, w_in, nsa_pos_k, nsa_pos_v, nsa_ck_w1, nsa_ck_b1, nsa_ck_w2, nsa_cv_w1, nsa_cv_b1, nsa_cv_w2, mla_g_q, mla_w_uq, mla_g_kv, mla_w_ukv, w_br_nsa, w_br_mla, w_o, g_post_mix, g_pre_ffn, w_up, w_conv, b_conv, w_down, g_post_ffn, w_ple, w_ple_gate, g_ple):
    b, t, d = x.shape
    depth = w_in.shape[0]
    assert d == D_MODEL and t // SEL_BLOCK == LANES - SEL_LANE0 and t % PROJ_ROWS == 0
    n = b * t
    qtab, ktab, ctab = _position_tables(t)
    ropetab = _rope_table(positions)
    place = _rope_placement()
    xc = x.reshape(n, d)
    for i in range(depth):
        w_all = _layout_w_in(w_in[i])
        wuq = mla_w_uq[i].reshape(Q_LORA, MLA_HEADS, QK_NOPE + QK_ROPE)
        wuq = jnp.concatenate([wuq, _swap_halves(wuq[..., QK_NOPE:])], axis=-1).reshape(Q_LORA, MLA_HEADS * LANES)
        wukv = mla_w_ukv[i].reshape(KV_LORA, MLA_HEADS, QK_NOPE + V_DIM)
        wk = _pad_heads(wukv[..., :QK_NOPE].reshape(KV_LORA, -1), MLA_HEADS, QK_NOPE)
        wv = _pad_heads(wukv[..., QK_NOPE:].reshape(KV_LORA, -1), MLA_HEADS, V_DIM)
        (q_n, kc, vc, ks, vs, kw, vw, gs, q_m, k_m, v_m, gm) = _proj_call(
            xc, g_pre_mix[i][None], w_all, wuq.astype(BF16), mla_g_q[i][None], wk.astype(BF16), wv.astype(BF16),
            mla_g_kv[i][None], place, qtab, ktab, ropetab, t)

        n_sb = t // SEL_BLOCK
        kcmp, vcmp = _compress_call(
            kc.reshape(b, n_sb, SEL_BLOCK * LANES), vc.reshape(b, n_sb, SEL_BLOCK * LANES),
            *_layout_compress(nsa_ck_w1[i], nsa_pos_k[i], nsa_ck_b1[i], nsa_ck_w2[i]),
            *_layout_compress(nsa_cv_w1[i], nsa_pos_v[i], nsa_cv_b1[i], nsa_cv_w2[i]), ctab)

        def seq3(a):
            return a.reshape(b, t, a.shape[-1])

        o_nsa = _nsa_call(seq3(q_n), kcmp, vcmp, seq3(ks), seq3(vs), seq3(kw), seq3(vw), seq3(gs))
        o_mla = _mla_call(q_m, k_m, v_m)

        x1, h2 = _merge_call(o_nsa.reshape(n, NSA_WIDTH), o_mla, gm, xc, w_br_nsa[i].astype(BF16),
                             w_br_mla[i].astype(BF16).reshape(-1, LANES, D_MODEL), w_o[i].astype(BF16),
                             g_post_mix[i][None], g_pre_ffn[i][None])
        xc = _ffn_call(h2, x1, p[i].reshape(n, PLE_DIM), w_up[i].astype(BF16), w_conv[i], b_conv[i][None],
                       w_down[i].astype(BF16), g_post_ffn[i][None], w_ple[i].astype(BF16),
                       w_ple_gate[i].astype(BF16), g_ple[i][None], t)
    return xc.reshape(b, t, d)
```

```python
import functools
import math

import numpy as np
import jax
import jax.numpy as jnp
from jax import lax
from jax.experimental import pallas as pl
from jax.experimental.pallas import tpu as pltpu

F32 = jnp.float32
BF16 = jnp.bfloat16

D_MODEL = 1024
PLE_DIM = 256
NSA_HEADS = 8
NSA_KV_GROUPS = 2
NSA_REP = NSA_HEADS // NSA_KV_GROUPS
NSA_HEAD_DIM = 64
NSA_WIDTH = NSA_HEADS * NSA_HEAD_DIM
CMP_BLOCK = 32
CMP_STRIDE = 16
SEL_BLOCK = 64
N_SEL = 16
WINDOW = 512
MLA_HEADS = 8
Q_LORA = 256
KV_LORA = 128
QK_NOPE = 64
QK_ROPE = 32
V_DIM = 64
MLA_WIDTH = MLA_HEADS * V_DIM
ROPE_THETA = 10000.0
D_FF = 2816
CONV_WIDTH = 3
EPS = 1e-6
NEG = -1e30
BIG = 1e9
ALIBI_SLOPES = tuple(2.0 ** (-8.0 * (i + 1) / NSA_HEADS) for i in range(NSA_HEADS))

LANES = 128
SUBLANES = 8

FEAT = NSA_HEAD_DIM
SEL_LANE0 = 96
ONES_LANE = NSA_HEAD_DIM

PROJ_ROWS = 512
NSA_TQ = 128
NSA_SEL_SLAB = 512
MLA_TQ = 256
MLA_SLAB = 512
FFN_ROWS = 512
FFN_CHUNK = 256
FFN_GROUP = 4

_C_Q = 0
_C_KC = _C_Q + NSA_WIDTH
_C_VC = _C_KC + LANES
_C_KS = _C_VC + LANES
_C_VS = _C_KS + LANES
_C_KW = _C_VS + LANES
_C_VW = _C_KW + LANES
_C_SMALL = _C_VW + LANES
_C_CQ = _C_SMALL + LANES
_C_CKV = _C_CQ + Q_LORA
_C_MERGE = _C_CKV + KV_LORA
_C_END = _C_MERGE + 2 * D_MODEL
_SMALL_KR = 32


def _vmem_params(semantics, mib):
    return pltpu.CompilerParams(dimension_semantics=semantics, vmem_limit_bytes=mib * 1024 * 1024)


def _const_spec(shape):
    nd = len(shape)
    return pl.BlockSpec(shape, lambda *_: (0,) * nd, pipeline_mode=pl.Buffered(1))


def _rms(x, g):
    return x * lax.rsqrt(jnp.mean(x * x, axis=-1, keepdims=True) + EPS) * g


def _gelu_tanh(x):
    return 0.5 * x * (1.0 + jnp.tanh(math.sqrt(2.0 / math.pi) * (x + 0.044715 * (x * x * x))))


def _dot(a, b):
    return jnp.dot(a, b, preferred_element_type=F32)


def _dot_nt(a, b):
    return lax.dot_general(a, b, (((1,), (1,)), ((), ())), preferred_element_type=F32)


def _dot_tn(a, b):
    return lax.dot_general(a, b, (((0,), (0,)), ((), ())), preferred_element_type=F32)


def _proj_kernel(x_ref, g_ref, w_ref, wuq_ref, gq_ref, wk_ref, wv_ref, gkv_ref, place_ref, qtab_ref, ktab_ref,
                 rope_ref, q_out, kc_out, vc_out, ks_out, vs_out, kw_out, vw_out, gs_out, qm_out, km_out,
                 vm_out, gm_out):
    rows = x_ref.shape[0]
    h = _rms(x_ref[...], g_ref[...]).astype(BF16)

    def proj(a, b):
        return _dot(h, w_ref[:, a:b])

    lane = lax.broadcasted_iota(jnp.int32, (rows, LANES), 1)
    ones_col = (lane == ONES_LANE).astype(F32)

    data = lane < NSA_HEAD_DIM

    def halves(col):
        packed = proj(col, col + LANES)
        return packed, pltpu.roll(packed, LANES - NSA_HEAD_DIM, 1)

    qtab = qtab_ref[...]
    for pr in range(NSA_HEADS // 2):
        for hd, qh in zip((2 * pr, 2 * pr + 1), halves(_C_Q + pr * LANES)):
            q_out[:, hd * LANES:(hd + 1) * LANES] = jnp.where(data, qh, ALIBI_SLOPES[hd] * qtab).astype(BF16)

    kc_out[...] = proj(_C_KC, _C_KC + LANES).astype(BF16)
    vc_out[...] = proj(_C_VC, _C_VC + LANES).astype(BF16)

    ktab_sel = ktab_ref[...]
    ktab_win = jnp.where(lane < SEL_LANE0, ktab_sel, 0.0)
    for col, out, feat in ((_C_KS, ks_out, ktab_sel), (_C_VS, vs_out, ones_col), (_C_KW, kw_out, ktab_win),
                           (_C_VW, vw_out, ones_col)):
        for g, part in enumerate(halves(col)):
            out[:, g * LANES:(g + 1) * LANES] = jnp.where(data, part, feat).astype(BF16)

    small = proj(_C_SMALL, _C_SMALL + LANES)
    gs_out[...] = jax.nn.sigmoid(small).astype(BF16)

    mq = rope_ref[:, 0:LANES]
    rk = rope_ref[:, LANES:2 * LANES]
    t = small * rk
    k_rope = (t + pltpu.roll(t, LANES - QK_ROPE, 1)).astype(BF16)

    cqn = _rms(proj(_C_CQ, _C_CQ + Q_LORA), gq_ref[...]).astype(BF16)
    for hd in range(MLA_HEADS):
        sl = slice(hd * LANES, (hd + 1) * LANES)
        qm_out[0, hd] = (_dot(cqn, wuq_ref[:, sl]) * mq).astype(BF16)

    ckvn = _rms(proj(_C_CKV, _C_CKV + KV_LORA), gkv_ref[...]).astype(BF16)
    for hd in range(MLA_HEADS):
        sl = slice(hd * LANES, (hd + 1) * LANES)
        km_out[0, hd] = (_dot(ckvn, wk_ref[:, sl]) + _dot(k_rope, place_ref[:, sl])).astype(BF16)
        vm_out[0, hd] = (_dot(ckvn, wv_ref[:, sl]) + ones_col).astype(BF16)

    half = D_MODEL // 2
    for c in range(2 * D_MODEL // half):
        a = _C_MERGE + c * half
        gm_out[:, c * half:(c + 1) * half] = jax.nn.sigmoid(proj(a, a + half)).astype(BF16)


def _proj_call(x2d, g_pre, w_all, wuq, g_q, wk, wv, g_kv, place, qtab, ktab, ropetab, seq):
    n = x2d.shape[0]
    rows = PROJ_ROWS
    per_seq = seq // rows

    def row_spec(width):
        return pl.BlockSpec((rows, width), lambda i: (i, 0))

    tab_spec = pl.BlockSpec((rows, LANES), lambda i: (i % per_seq, 0))
    head_spec = pl.BlockSpec((1, MLA_HEADS, rows, LANES), lambda i: (i // per_seq, 0, i % per_seq, 0))
    head_shape = jax.ShapeDtypeStruct((n // seq, MLA_HEADS, seq, LANES), BF16)
    widths = (NSA_HEADS * LANES, LANES, LANES, NSA_KV_GROUPS * LANES, NSA_KV_GROUPS * LANES,
              NSA_KV_GROUPS * LANES, NSA_KV_GROUPS * LANES, LANES)
    return pl.pallas_call(
        _proj_kernel,
        grid=(n // rows,),
        in_specs=[row_spec(D_MODEL), _const_spec(g_pre.shape), _const_spec(w_all.shape), _const_spec(wuq.shape),
                  _const_spec(g_q.shape), _const_spec(wk.shape), _const_spec(wv.shape), _const_spec(g_kv.shape),
                  _const_spec(place.shape), tab_spec, tab_spec, row_spec(2 * LANES)],
        out_specs=[row_spec(w) for w in widths] + [head_spec] * 3 + [row_spec(2 * D_MODEL)],
        out_shape=[jax.ShapeDtypeStruct((n, w), BF16) for w in widths] + [head_shape] * 3
        + [jax.ShapeDtypeStruct((n, 2 * D_MODEL), BF16)],
        compiler_params=_vmem_params(("parallel",), 48),
        name="proj",
    )(x2d, g_pre, w_all, wuq, g_q, wk, wv, g_kv, place, qtab, ktab, ropetab)


def _compress_kernel(kx_ref, vx_ref, w1k_ref, posk_ref, b1k_ref, w2k_ref, w1v_ref, posv_ref, b1v_ref, w2v_ref,
                     ktab_ref, kc_out, vc_out):
    n_sb = kx_ref.shape[1]
    half = w1k_ref.shape[0] // 2
    per = SEL_BLOCK // CMP_STRIDE

    def one(x_ref, w1_ref, pos_ref, b1_ref, w2_ref):
        x = x_ref[0].astype(F32)
        tops, bots = [], []
        for c in range(per):
            xc = x[:, c * half:(c + 1) * half]
            tops.append(_dot((xc + pos_ref[:, 0:half]).astype(BF16), w1_ref[0:half, :]))
            bots.append(_dot((xc + pos_ref[:, half:2 * half]).astype(BF16), w1_ref[half:2 * half, :]))
        nxt = pltpu.roll(bots[0], n_sb - 1, 0)
        row = lax.broadcasted_iota(jnp.int32, nxt.shape, 0)
        nxt = jnp.where(row == n_sb - 1, 0.0, nxt)
        pre = jnp.concatenate([tops[c] + (bots[c + 1] if c + 1 < per else nxt) for c in range(per)], axis=0)
        return _dot(_gelu_tanh(pre + b1_ref[...]).astype(BF16), w2_ref[...])

    kc_out[0] = (one(kx_ref, w1k_ref, posk_ref, b1k_ref, w2k_ref) + ktab_ref[...]).astype(BF16)
    vc_out[0] = one(vx_ref, w1v_ref, posv_ref, b1v_ref, w2v_ref).astype(BF16)


def _compress_call(kx, vx, w1k, posk, b1k, w2k, w1v, posv, b1v, w2v, ktab):
    b, n_sb, width = kx.shape
    n_rows = n_sb * (SEL_BLOCK // CMP_STRIDE)
    x_spec = pl.BlockSpec((1, n_sb, width), lambda i: (i, 0, 0))
    o_spec = pl.BlockSpec((1, n_rows, NSA_KV_GROUPS * LANES), lambda i: (i, 0, 0))
    consts = (w1k, posk, b1k, w2k, w1v, posv, b1v, w2v, ktab)
    return pl.pallas_call(
        _compress_kernel,
        grid=(b,),
        in_specs=[x_spec, x_spec] + [_const_spec(c.shape) for c in consts],
        out_specs=[o_spec, o_spec],
        out_shape=[jax.ShapeDtypeStruct((b, n_rows, NSA_KV_GROUPS * LANES), BF16)] * 2,
        compiler_params=_vmem_params(("parallel",), 32),
        name="compress",
    )(kx, vx, *consts)


def _normalize(acc):
    return acc * (1.0 / acc[:, ONES_LANE:ONES_LANE + 1])


def _dense_attend(q, tiles, k_at, v_at, s_ref):
    maxes = []
    off = 0
    for start, width, mask in tiles:
        s = _dot_nt(q, k_at(start, width))
        if mask is not None:
            s = jnp.where(mask, s, NEG)
        s_ref[:, off:off + width] = s
        maxes.append(jnp.max(s, axis=1, keepdims=True))
        off += width
    m = functools.reduce(jnp.maximum, maxes)
    acc = None
    off = 0
    for start, width, _ in tiles:
        part = _dot(jnp.exp(s_ref[:, off:off + width] - m).astype(BF16), v_at(start, width))
        acc = part if acc is None else acc + part
        off += width
    return _normalize(acc)


def _split_tiles(start, total, width):
    return [(start + o, min(width, total - o), None) for o in range(0, total, width)]


def _nsa_kernel(q_ref, kcmp_ref, vcmp_ref, ks_ref, vs_ref, kw_ref, vw_ref, gs_ref, o_ref, qsel_ref, ocmp_ref,
                osel_ref, owin_ref, s_ref):
    tq = q_ref.shape[1]
    seq = ks_ref.shape[1]
    qi = pl.program_id(1)
    t0 = qi * tq
    m_rows = NSA_REP * tq
    n_cmp_rows = kcmp_ref.shape[1]
    n_sb = n_cmp_rows // (SEL_BLOCK // CMP_STRIDE)
    per = SEL_BLOCK // CMP_STRIDE
    win_tiles = WINDOW // tq
    groups = [slice(g * LANES, (g + 1) * LANES) for g in range(NSA_KV_GROUPS)]

    def q_group(g):
        return jnp.concatenate(
            [q_ref[0, :, (g * NSA_REP + r) * LANES:(g * NSA_REP + r + 1) * LANES] for r in range(NSA_REP)], axis=0)

    def rel(width):
        return (lax.broadcasted_iota(jnp.int32, (m_rows, width), 1)
                - lax.broadcasted_iota(jnp.int32, (m_rows, width), 0) % tq)

    lane_q = lax.broadcasted_iota(jnp.int32, (m_rows, LANES), 1)
    for g, gl in enumerate(groups):
        qg = q_group(g)

        s_t = _dot_nt(kcmp_ref[0, :, gl], qg)
        krow = lax.broadcasted_iota(jnp.int32, s_t.shape, 0)
        blk_end = (per * (krow % n_sb) + krow // n_sb) * CMP_STRIDE + (CMP_BLOCK - 1)
        t_abs = t0 + lax.broadcasted_iota(jnp.int32, s_t.shape, 1) % tq
        cmask = t_abs >= blk_end
        s_t = jnp.where(cmask, s_t, NEG)
        e = jnp.exp(s_t - jnp.max(s_t, axis=0, keepdims=True))
        p_t = jnp.where(cmask, e, 0.0) * (1.0 / jnp.sum(e, axis=0, keepdims=True))
        o_cmp = _dot_tn(p_t.astype(BF16), vcmp_ref[0, :, gl])

        imp = None
        for r in range(NSA_REP):
            for c in range(per):
                piece = p_t[c * n_sb:(c + 1) * n_sb, r * tq:(r + 1) * tq]
                imp = piece if imp is None else imp + piece
        sb = lax.broadcasted_iota(jnp.int32, (n_sb, tq), 0)
        cur = (t0 + lax.broadcasted_iota(jnp.int32, (n_sb, tq), 1)) // SEL_BLOCK
        forced = (sb == 0) | (sb == cur) | (sb == cur - 1)
        score = jnp.where(forced, BIG, jnp.where(sb > cur, -BIG, imp))
        rank = jnp.zeros((n_sb, tq), jnp.int32)
        for j in range(n_sb):
            sj = score[j:j + 1, :]
            before = (sj > score) | ((sj == score) & (sb > j))
            rank = rank + before.astype(jnp.int32)
        sel_neg = jnp.where(rank < min(N_SEL, n_sb), 0.0, NEG)
        sel_q = jnp.concatenate([jnp.zeros((SEL_LANE0, tq), F32), sel_neg], axis=0).T.astype(BF16)
        qsel_ref[g] = jnp.where(lane_q >= SEL_LANE0, jnp.concatenate([sel_q] * NSA_REP, axis=0), qg)
        ocmp_ref[g] = o_cmp

    def kv_at(k_ref, gl):
        return lambda start, width: k_ref[0, pl.ds(start, width), gl]

    slab = NSA_SEL_SLAB
    for n in range(seq // slab):

        @pl.when(qi // (slab // tq) == n)
        def _(n=n):
            last = rel(slab) <= t0 - n * slab
            tiles = _split_tiles(0, n * slab, slab) + [(n * slab, slab, last)]
            for g, gl in enumerate(groups):
                osel_ref[g] = _dense_attend(qsel_ref[g], tiles, kv_at(ks_ref, gl), kv_at(vs_ref, gl), s_ref.at[g])

    for v in range(win_tiles + 1):

        @pl.when((qi == v) if v < win_tiles else (qi >= win_tiles))
        def _(v=v):
            causal = rel(tq) <= 0
            if v < win_tiles:
                tiles = _split_tiles(0, v * tq, WINDOW) + [(v * tq, tq, causal)]
            else:
                first = pl.multiple_of((qi - win_tiles) * tq, tq)
                tiles = [(first, tq, rel(tq) > 0), (pl.multiple_of(first + tq, tq), WINDOW - tq, None),
                         (pl.multiple_of(t0, tq), tq, causal)]
            for g, gl in enumerate(groups):
                owin_ref[g] = _dense_attend(q_group(g), tiles, kv_at(kw_ref, gl), kv_at(vw_ref, gl), s_ref.at[g])

    gates = gs_ref[0].astype(F32)
    outs = []
    for g in range(NSA_KV_GROUPS):
        for r in range(NSA_REP):
            hd = g * NSA_REP + r
            rs = slice(r * tq, (r + 1) * tq)
            o = (gates[:, hd:hd + 1] * ocmp_ref[g, rs, :]
                 + gates[:, NSA_HEADS + hd:NSA_HEADS + hd + 1] * osel_ref[g, rs, :]
                 + gates[:, 2 * NSA_HEADS + hd:2 * NSA_HEADS + hd + 1] * owin_ref[g, rs, :])
            outs.append(o[:, 0:NSA_HEAD_DIM])
    o_ref[0] = jnp.concatenate(outs, axis=1).astype(o_ref.dtype)


def _nsa_call(q, kcmp, vcmp, ks, vs, kw, vw, gs):
    b, t, _ = q.shape
    tq = NSA_TQ
    m_rows = NSA_REP * tq
    branch_out = pltpu.VMEM((NSA_KV_GROUPS, m_rows, LANES), F32)
    kv_spec = pl.BlockSpec((1, t, NSA_KV_GROUPS * LANES), lambda i, j: (i, 0, 0))
    cmp_spec = pl.BlockSpec((1, kcmp.shape[1], NSA_KV_GROUPS * LANES), lambda i, j: (i, 0, 0))
    return pl.pallas_call(
        _nsa_kernel,
        grid=(b, t // tq),
        in_specs=[pl.BlockSpec((1, tq, NSA_HEADS * LANES), lambda i, j: (i, j, 0)), cmp_spec, cmp_spec,
                  kv_spec, kv_spec, kv_spec, kv_spec, pl.BlockSpec((1, tq, LANES), lambda i, j: (i, j, 0))],
        out_specs=pl.BlockSpec((1, tq, NSA_WIDTH), lambda i, j: (i, j, 0)),
        out_shape=jax.ShapeDtypeStruct((b, t, NSA_WIDTH), BF16),
        scratch_shapes=[pltpu.VMEM((NSA_KV_GROUPS, m_rows, LANES), BF16), branch_out, branch_out, branch_out,
                        pltpu.VMEM((NSA_KV_GROUPS, m_rows, t), F32)],
        compiler_params=_vmem_params(("parallel", "arbitrary"), 48),
        name="nsa",
    )(q, kcmp, vcmp, ks, vs, kw, vw, gs)


def _mla_kernel(qlo_ref, qhi_ref, k_ref, v_ref, o_ref, s_ref):
    tq = qlo_ref.shape[2]
    nq = k_ref.shape[2] // tq
    step = pl.program_id(1)
    pairs = o_ref.shape[1]
    per_pair = MLA_HEADS // pairs

    for n in range(nq // 2):

        @pl.when(step == n)
        def _(n=n):
            causal = (lax.broadcasted_iota(jnp.int32, (tq, tq), 1) <= lax.broadcasted_iota(jnp.int32, (tq, tq), 0))

            def pair_body(pr, carry):
                for which, (q_ref, qt) in enumerate(((qlo_ref, n), (qhi_ref, nq - 1 - n))):
                    tiles = _split_tiles(0, qt * tq, MLA_SLAB) + [(qt * tq, tq, causal)]
                    outs = []
                    for e in range(per_pair):
                        hd = per_pair * pr + e
                        o = _dense_attend(
                            q_ref[0, hd], tiles,
                            lambda start, width, hd=hd: k_ref[0, hd, pl.ds(start, width), :],
                            lambda start, width, hd=hd: v_ref[0, hd, pl.ds(start, width), :],
                            s_ref.at[which * per_pair + e])
                        outs.append(o[:, 0:V_DIM])
                    o_ref[0, pr, qt * tq:(qt + 1) * tq, :] = jnp.concatenate(outs, axis=1).astype(o_ref.dtype)
                return carry

            lax.fori_loop(0, pairs, pair_body, 0)


def _mla_call(q, k, v):
    b, heads, t, width = q.shape
    tq = MLA_TQ
    nq = t // tq
    pairs = heads * V_DIM // LANES
    kv_spec = pl.BlockSpec((1, heads, t, width), lambda i, j: (i, 0, 0, 0))
    return pl.pallas_call(
        _mla_kernel,
        grid=(b, nq // 2),
        in_specs=[pl.BlockSpec((1, heads, tq, width), lambda i, j: (i, 0, j, 0)),
                  pl.BlockSpec((1, heads, tq, width), lambda i, j: (i, 0, nq - 1 - j, 0)), kv_spec, kv_spec],
        out_specs=pl.BlockSpec((1, pairs, t, LANES), lambda i, j: (i, 0, 0, 0)),
        out_shape=jax.ShapeDtypeStruct((b, pairs, t, LANES), BF16),
        scratch_shapes=[pltpu.VMEM((2 * (heads // pairs), tq, t), F32)],
        compiler_params=_vmem_params(("parallel", "arbitrary"), 56),
        name="mla",
    )(q, q, k, v)


def _channel_kernel(on_ref, om_ref, gm_ref, x_ref, p_ref, wbn_ref, wbm_ref, wo_ref, gpost_ref, gpre_ref, wup_ref,
                    wconv_ref, bconv_ref, wdown_ref, g_ref, wple_ref, wpg_ref, gple_ref, out_ref, halo_ref, sa_ref,
                    sv_ref, gate_ref, x1_ref, h_ref, *, tiles_per_seq):
    rows = x_ref.shape[0]
    pad = SUBLANES
    keep = CONV_WIDTH - 1
    seq_start = pl.program_id(0) % tiles_per_seq == 0

    @pl.when(pl.program_id(0) == 0)
    def _():
        halo_ref[...] = jnp.zeros_like(halo_ref)

    a = _dot(on_ref[...], wbn_ref[...])
    b = _dot(om_ref[0, 0], wbm_ref[0])
    for pr in range(1, om_ref.shape[1]):
        b = b + _dot(om_ref[0, pr], wbm_ref[pr])
    merged = gm_ref[:, 0:D_MODEL].astype(F32) * a + gm_ref[:, D_MODEL:2 * D_MODEL].astype(F32) * b
    x1 = x_ref[...] + _rms(_dot(merged.astype(BF16), wo_ref[...]), gpost_ref[...])
    x1_ref[...] = x1
    h_ref[...] = _rms(x1, gpre_ref[...]).astype(BF16)
    h = h_ref[...]

    def conv_chunk(off, s_ref):
        cols = slice(off, off + FFN_CHUNK)
        u = _dot(h, wup_ref[:, cols])
        prev = jnp.where(seq_start, 0.0, halo_ref[pad - keep:pad, cols])
        s_ref[pad - keep:pad, :] = prev
        s_ref[pad:pad + rows, :] = u
        halo_ref[pad - keep:pad, cols] = s_ref[pad + rows - keep:pad + rows, :]
        y = wconv_ref[CONV_WIDTH - 1:CONV_WIDTH, cols] * u + bconv_ref[:, cols]
        for k in range(CONV_WIDTH - 1):
            shift = CONV_WIDTH - 1 - k
            y = y + wconv_ref[k:k + 1, cols] * s_ref[pad - shift:pad - shift + rows, :]
        return y

    n_chunks = D_FF // FFN_CHUNK
    acc = None
    for c0 in range(0, n_chunks, FFN_GROUP):
        chunks = range(c0, min(c0 + FFN_GROUP, n_chunks))
        slot = (c0 // FFN_GROUP) % 2
        for j, c in enumerate(chunks):
            a = conv_chunk(c * FFN_CHUNK, sa_ref.at[c % 2])
            v = conv_chunk(D_FF + c * FFN_CHUNK, sv_ref.at[c % 2])
            gate_ref[slot, :, j * FFN_CHUNK:(j + 1) * FFN_CHUNK] = (_gelu_tanh(a) * v).astype(BF16)
        width = len(chunks) * FFN_CHUNK
        part = _dot(gate_ref[slot, :, 0:width], wdown_ref[c0 * FFN_CHUNK:c0 * FFN_CHUNK + width, :])
        acc = part if acc is None else acc + part
    x2 = x1_ref[...] + _rms(acc, g_ref[...])
    e = _dot(p_ref[...].astype(BF16), wple_ref[...])
    gate = jax.nn.sigmoid(_dot(x2.astype(BF16), wpg_ref[...]))
    out_ref[...] = x2 + _rms(e * gate, gple_ref[...])


def _channel_call(o_nsa, o_mla, gm, x2d, p2d, wbn, wbm, wo, g_post, g_pre_ffn, wup, wconv, bconv, wdown, g_post_ffn,
                  wple, wgate, g_ple):
    n = x2d.shape[0]
    rows = FFN_ROWS
    _, pairs, seq, _ = o_mla.shape
    per_seq = seq // rows

    def row_spec(width=D_MODEL):
        return pl.BlockSpec((rows, width), lambda i: (i, 0))

    mla_spec = pl.BlockSpec((1, pairs, rows, LANES), lambda i: (i // per_seq, 0, i % per_seq, 0))
    consts = (wbn, wbm, wo, g_post, g_pre_ffn, wup, wconv, bconv, wdown, g_post_ffn, wple, wgate, g_ple)
    return pl.pallas_call(
        functools.partial(_channel_kernel, tiles_per_seq=per_seq),
        grid=(n // rows,),
        in_specs=[row_spec(NSA_WIDTH), mla_spec, row_spec(2 * D_MODEL), row_spec(), row_spec(PLE_DIM)]
        + [_const_spec(c.shape) for c in consts],
        out_specs=row_spec(),
        out_shape=jax.ShapeDtypeStruct((n, D_MODEL), F32),
        scratch_shapes=[pltpu.VMEM((SUBLANES, 2 * D_FF), F32),
                        pltpu.VMEM((2, rows + SUBLANES, FFN_CHUNK), F32),
                        pltpu.VMEM((2, rows + SUBLANES, FFN_CHUNK), F32),
                        pltpu.VMEM((2, rows, FFN_GROUP * FFN_CHUNK), BF16),
                        pltpu.VMEM((rows, D_MODEL), F32),
                        pltpu.VMEM((rows, D_MODEL), BF16)],
        compiler_params=_vmem_params(("arbitrary",), 56),
        name="channel",
    )(o_nsa, o_mla, gm, x2d, p2d, *consts)


def _pad_heads(w, heads, width):
    rows = w.shape[0]
    w = w.reshape(rows, heads, width)
    return jnp.pad(w, ((0, 0), (0, 0), (0, LANES - width))).reshape(rows, heads * LANES)


def _swap_halves(w):
    half = w.shape[-1] // 2
    return jnp.concatenate([w[..., half:], w[..., :half]], axis=-1)


def _layout_w_in(w_in):
    sizes = (NSA_WIDTH, 6 * NSA_KV_GROUPS * NSA_HEAD_DIM, 3 * NSA_HEADS, Q_LORA, KV_LORA, QK_ROPE, 2 * D_MODEL)
    offs = np.cumsum((0,) + sizes)
    wq, wkv, wgate, wcq, wckv, wkr, wmerge = (w_in[:, offs[i]:offs[i + 1]] for i in range(len(sizes)))
    rows = w_in.shape[0]
    small = jnp.concatenate([wgate, jnp.zeros((rows, _SMALL_KR - wgate.shape[1]), w_in.dtype), wkr, _swap_halves(wkr),
                             jnp.zeros((rows, LANES - _SMALL_KR - 2 * QK_ROPE), w_in.dtype)], axis=1)
    parts = [wq * (NSA_HEAD_DIM ** -0.5), wkv, small, wcq, wckv, wmerge]
    w_all = jnp.concatenate([part.astype(BF16) for part in parts], axis=1)
    assert w_all.shape[1] == _C_END
    return w_all


def _layout_compress(w1, pos, b1, w2):
    dh, g, half_tok = NSA_HEAD_DIM, NSA_KV_GROUPS, CMP_BLOCK // 2
    eye = jnp.eye(g, dtype=w1.dtype)
    w1 = w1.reshape(CMP_BLOCK, dh, dh)
    w1e = jnp.einsum("lde,gh->lgdhe", w1, eye).reshape(CMP_BLOCK * g * dh, g * dh)
    pose = jnp.broadcast_to(pos[:, None, :], (CMP_BLOCK, g, dh)).reshape(1, CMP_BLOCK * g * dh)
    b1e = jnp.tile(b1, g).reshape(1, g * dh)
    w2e = jnp.einsum("de,gh->gdhe", w2, eye)
    w2e = jnp.pad(w2e, ((0, 0), (0, 0), (0, 0), (0, LANES - dh))).reshape(g * dh, g * LANES)
    del half_tok
    return w1e.astype(BF16), pose.astype(F32), b1e.astype(F32), w2e.astype(BF16)


def _position_tables(seq):
    t = np.arange(seq)
    hi = (t // SEL_BLOCK) * SEL_BLOCK
    lo = t % SEL_BLOCK
    qtab = np.zeros((seq, LANES), np.float32)
    qtab[:, FEAT + 0] = -hi
    qtab[:, FEAT + 1] = -lo
    qtab[:, FEAT + 2] = 1.0
    qtab[:, FEAT + 3] = 1.0
    ktab = np.zeros((seq, LANES), np.float32)
    ktab[:, FEAT + 0] = 1.0
    ktab[:, FEAT + 1] = 1.0
    ktab[:, FEAT + 2] = hi
    ktab[:, FEAT + 3] = lo
    ktab[t, SEL_LANE0 + t // SEL_BLOCK] = 1.0
    per = SEL_BLOCK // CMP_STRIDE
    n_sb = seq // SEL_BLOCK
    r = np.arange(per * n_sb)
    end = (per * (r % n_sb) + r // n_sb) * CMP_STRIDE + CMP_BLOCK - 1
    ctab = np.zeros((per * n_sb, NSA_KV_GROUPS, LANES), np.float32)
    ctab[:, :, FEAT + 0] = 1.0
    ctab[:, :, FEAT + 1] = 1.0
    ctab[:, :, FEAT + 2] = ((end // SEL_BLOCK) * SEL_BLOCK)[:, None]
    ctab[:, :, FEAT + 3] = (end % SEL_BLOCK)[:, None]
    return jnp.asarray(qtab), jnp.asarray(ktab), jnp.asarray(ctab.reshape(per * n_sb, NSA_KV_GROUPS * LANES))


def _rope_table(positions):
    inv_freq = ROPE_THETA ** (-jnp.arange(0, QK_ROPE, 2, dtype=F32) / QK_ROPE)
    ang = positions.astype(F32)[..., None] * inv_freq
    cos, sin = jnp.cos(ang), jnp.sin(ang)
    c2 = jnp.concatenate([cos, cos], axis=-1)
    s2 = jnp.concatenate([-sin, sin], axis=-1)
    scale = (QK_NOPE + QK_ROPE) ** -0.5
    lead = c2.shape[:-1]
    mq = jnp.concatenate([jnp.full(lead + (QK_NOPE,), scale, F32), scale * c2, scale * s2], axis=-1)
    rk = jnp.concatenate([jnp.zeros(lead + (_SMALL_KR,), F32), c2, s2,
                          jnp.zeros(lead + (LANES - _SMALL_KR - 2 * QK_ROPE,), F32)], axis=-1)
    return jnp.concatenate([mq, rk], axis=-1).reshape(-1, 2 * LANES)


def _rope_placement():
    place = np.zeros((LANES, MLA_HEADS, LANES), np.float32)
    i = np.arange(QK_ROPE)
    place[_SMALL_KR + i, :, QK_NOPE + i] = 1.0
    place[_SMALL_KR + i, :, QK_NOPE + QK_ROPE + i] = 1.0
    return jnp.asarray(place.reshape(LANES, MLA_HEADS * LANES), dtype=BF16)


def kernel(x, p, positions, g_pre_mix, w_in, nsa_pos_k, nsa_pos_v, nsa_ck_w1, nsa_ck_b1, nsa_ck_w2, nsa_cv_w1, nsa_cv_b1, nsa_cv_w2, mla_g_q, mla_w_uq, mla_g_kv, mla_w_ukv, w_br_nsa, w_br_mla, w_o, g_post_mix, g_pre_ffn, w_up, w_conv, b_conv, w_down, g_post_ffn, w_ple, w_ple_gate, g_ple):
    b, t, d = x.shape
    depth = w_in.shape[0]
    assert d == D_MODEL and t // SEL_BLOCK == LANES - SEL_LANE0 and t % PROJ_ROWS == 0
    n = b * t
    qtab, ktab, ctab = _position_tables(t)
    ropetab = _rope_table(positions)
    place = _rope_placement()
    xc = x.reshape(n, d)
    for i in range(depth):
        w_all = _layout_w_in(w_in[i])
        wuq = mla_w_uq[i].reshape(Q_LORA, MLA_HEADS, QK_NOPE + QK_ROPE)
        wuq = jnp.concatenate([wuq, _swap_halves(wuq[..., QK_NOPE:])], axis=-1).reshape(Q_LORA, MLA_HEADS * LANES)
        wukv = mla_w_ukv[i].reshape(KV_LORA, MLA_HEADS, QK_NOPE + V_DIM)
        wk = _pad_heads(wukv[..., :QK_NOPE].reshape(KV_LORA, -1), MLA_HEADS, QK_NOPE)
        wv = _pad_heads(wukv[..., QK_NOPE:].reshape(KV_LORA, -1), MLA_HEADS, V_DIM)
        (q_n, kc, vc, ks, vs, kw, vw, gs, q_m, k_m, v_m, gm) = _proj_call(
            xc, g_pre_mix[i][None], w_all, wuq.astype(BF16), mla_g_q[i][None], wk.astype(BF16), wv.astype(BF16),
            mla_g_kv[i][None], place, qtab, ktab, ropetab, t)

        n_sb = t // SEL_BLOCK
        kcmp, vcmp = _compress_call(
            kc.reshape(b, n_sb, SEL_BLOCK * LANES), vc.reshape(b, n_sb, SEL_BLOCK * LANES),
            *_layout_compress(nsa_ck_w1[i], nsa_pos_k[i], nsa_ck_b1[i], nsa_ck_w2[i]),
            *_layout_compress(nsa_cv_w1[i], nsa_pos_v[i], nsa_cv_b1[i], nsa_cv_w2[i]), ctab)

        def seq3(a):
            return a.reshape(b, t, a.shape[-1])

        o_nsa = _nsa_call(seq3(q_n), kcmp, vcmp, seq3(ks), seq3(vs), seq3(kw), seq3(vw), seq3(gs))
        o_mla = _mla_call(q_m, k_m, v_m)

        xc = _channel_call(o_nsa.reshape(n, NSA_WIDTH), o_mla, gm, xc, p[i].reshape(n, PLE_DIM),
                           w_br_nsa[i].astype(BF16), w_br_mla[i].astype(BF16).reshape(-1, LANES, D_MODEL),
                           w_o[i].astype(BF16), g_post_mix[i][None], g_pre_ffn[i][None], w_up[i].astype(BF16),
                           w_conv[i], b_conv[i][None], w_down[i].astype(BF16), g_post_ffn[i][None],
                           w_ple[i].astype(BF16), w_ple_gate[i].astype(BF16), g_ple[i][None])
    return xc.reshape(b, t, d)
```

```python
import functools
import math

import numpy as np
import jax
import jax.numpy as jnp
from jax import lax
from jax.experimental import pallas as pl
from jax.experimental.pallas import tpu as pltpu

F32 = jnp.float32
BF16 = jnp.bfloat16

D_MODEL = 1024
PLE_DIM = 256
NSA_HEADS = 8
NSA_KV_GROUPS = 2
NSA_REP = NSA_HEADS // NSA_KV_GROUPS
NSA_HEAD_DIM = 64
NSA_WIDTH = NSA_HEADS * NSA_HEAD_DIM
CMP_BLOCK = 32
CMP_STRIDE = 16
SEL_BLOCK = 64
N_SEL = 16
WINDOW = 512
MLA_HEADS = 8
Q_LORA = 256
KV_LORA = 128
QK_NOPE = 64
QK_ROPE = 32
V_DIM = 64
MLA_WIDTH = MLA_HEADS * V_DIM
ROPE_THETA = 10000.0
D_FF = 2816
CONV_WIDTH = 3
EPS = 1e-6
NEG = -1e30
BIG = 1e9
ALIBI_SLOPES = tuple(2.0 ** (-8.0 * (i + 1) / NSA_HEADS) for i in range(NSA_HEADS))

LANES = 128
SUBLANES = 8

FEAT = NSA_HEAD_DIM
SEL_LANE0 = 96
ONES_LANE = NSA_HEAD_DIM

PROJ_ROWS = 512
NSA_TQ = 128
NSA_SEL_SLAB = 512
MLA_TQ = 256
MLA_SLAB = 512
FFN_ROWS = 512
FFN_CHUNK = 256
FFN_GROUP = 6

_C_Q = 0
_C_KC = _C_Q + NSA_WIDTH
_C_VC = _C_KC + LANES
_C_KS = _C_VC + LANES
_C_VS = _C_KS + LANES
_C_KW = _C_VS + LANES
_C_VW = _C_KW + LANES
_C_SMALL = _C_VW + LANES
_C_CQ = _C_SMALL + LANES
_C_CKV = _C_CQ + Q_LORA
_C_END = _C_CKV + KV_LORA
_SMALL_KR = 32


def _vmem_params(semantics, mib):
    return pltpu.CompilerParams(dimension_semantics=semantics, vmem_limit_bytes=mib * 1024 * 1024)


def _const_spec(shape):
    nd = len(shape)
    return pl.BlockSpec(shape, lambda *_: (0,) * nd, pipeline_mode=pl.Buffered(1))


def _rms(x, g):
    return x * lax.rsqrt(jnp.mean(x * x, axis=-1, keepdims=True) + EPS) * g


def _gelu_tanh(x):
    return 0.5 * x * (1.0 + jnp.tanh(math.sqrt(2.0 / math.pi) * (x + 0.044715 * (x * x * x))))


def _dot(a, b):
    return jnp.dot(a, b, preferred_element_type=F32)


def _dot_nt(a, b):
    return lax.dot_general(a, b, (((1,), (1,)), ((), ())), preferred_element_type=F32)


def _dot_tn(a, b):
    return lax.dot_general(a, b, (((0,), (0,)), ((), ())), preferred_element_type=F32)


def _proj_kernel(x_ref, g_ref, w_ref, wm_ref, wuq_ref, gq_ref, wk_ref, wv_ref, gkv_ref, place_ref, qtab_ref,
                 ktab_ref, rope_ref, q_out, kc_out, vc_out, ks_out, vs_out, kw_out, vw_out, gs_out, qm_out, km_out,
                 vm_out, gm_out):
    rows = x_ref.shape[0]
    h = _rms(x_ref[...], g_ref[...]).astype(BF16)

    def proj(a, b):
        return _dot(h, w_ref[:, a:b])

    lane = lax.broadcasted_iota(jnp.int32, (rows, LANES), 1)
    ones_col = (lane == ONES_LANE).astype(F32)

    data = lane < NSA_HEAD_DIM

    def halves(col):
        packed = proj(col, col + LANES)
        return packed, pltpu.roll(packed, LANES - NSA_HEAD_DIM, 1)

    qtab = qtab_ref[...]
    for pr in range(NSA_HEADS // 2):
        for hd, qh in zip((2 * pr, 2 * pr + 1), halves(_C_Q + pr * LANES)):
            q_out[:, hd * LANES:(hd + 1) * LANES] = jnp.where(data, qh, ALIBI_SLOPES[hd] * qtab).astype(BF16)

    kc_out[...] = proj(_C_KC, _C_KC + LANES).astype(BF16)
    vc_out[...] = proj(_C_VC, _C_VC + LANES).astype(BF16)

    ktab_sel = ktab_ref[...]
    ktab_win = jnp.where(lane < SEL_LANE0, ktab_sel, 0.0)
    for col, out, feat in ((_C_KS, ks_out, ktab_sel), (_C_VS, vs_out, ones_col), (_C_KW, kw_out, ktab_win),
                           (_C_VW, vw_out, ones_col)):
        for g, part in enumerate(halves(col)):
            out[:, g * LANES:(g + 1) * LANES] = jnp.where(data, part, feat).astype(BF16)

    small = proj(_C_SMALL, _C_SMALL + LANES)
    gs_out[...] = jax.nn.sigmoid(small).astype(BF16)

    mq = rope_ref[:, 0:LANES]
    rk = rope_ref[:, LANES:2 * LANES]
    t = small * rk
    k_rope = (t + pltpu.roll(t, LANES - QK_ROPE, 1)).astype(BF16)

    cqn = _rms(proj(_C_CQ, _C_CQ + Q_LORA), gq_ref[...]).astype(BF16)
    for hd in range(MLA_HEADS):
        sl = slice(hd * LANES, (hd + 1) * LANES)
        qm_out[0, hd] = (_dot(cqn, wuq_ref[:, sl]) * mq).astype(BF16)

    ckvn = _rms(proj(_C_CKV, _C_CKV + KV_LORA), gkv_ref[...]).astype(BF16)
    for hd in range(MLA_HEADS):
        sl = slice(hd * LANES, (hd + 1) * LANES)
        km_out[0, hd] = (_dot(ckvn, wk_ref[:, sl]) + _dot(k_rope, place_ref[:, sl])).astype(BF16)
        vm_out[0, hd] = (_dot(ckvn, wv_ref[:, sl]) + ones_col).astype(BF16)

    half = D_MODEL // 2
    for c in range(2 * D_MODEL // half):
        cols = slice(c * half, (c + 1) * half)
        gm_out[:, cols] = jax.nn.sigmoid(_dot(h, wm_ref[:, cols])).astype(BF16)


def _proj_call(x2d, g_pre, w_all, w_merge, wuq, g_q, wk, wv, g_kv, place, qtab, ktab, ropetab, seq):
    n = x2d.shape[0]
    rows = PROJ_ROWS
    per_seq = seq // rows

    def row_spec(width):
        return pl.BlockSpec((rows, width), lambda i: (i, 0))

    tab_spec = pl.BlockSpec((rows, LANES), lambda i: (i % per_seq, 0))
    head_spec = pl.BlockSpec((1, MLA_HEADS, rows, LANES), lambda i: (i // per_seq, 0, i % per_seq, 0))
    head_shape = jax.ShapeDtypeStruct((n // seq, MLA_HEADS, seq, LANES), BF16)
    widths = (NSA_HEADS * LANES, LANES, LANES, NSA_KV_GROUPS * LANES, NSA_KV_GROUPS * LANES,
              NSA_KV_GROUPS * LANES, NSA_KV_GROUPS * LANES, LANES)
    return pl.pallas_call(
        _proj_kernel,
        grid=(n // rows,),
        in_specs=[row_spec(D_MODEL), _const_spec(g_pre.shape), _const_spec(w_all.shape), _const_spec(w_merge.shape),
                  _const_spec(wuq.shape), _const_spec(g_q.shape), _const_spec(wk.shape), _const_spec(wv.shape),
                  _const_spec(g_kv.shape), _const_spec(place.shape), tab_spec, tab_spec, row_spec(2 * LANES)],
        out_specs=[row_spec(w) for w in widths] + [head_spec] * 3 + [row_spec(2 * D_MODEL)],
        out_shape=[jax.ShapeDtypeStruct((n, w), BF16) for w in widths] + [head_shape] * 3
        + [jax.ShapeDtypeStruct((n, 2 * D_MODEL), BF16)],
        compiler_params=_vmem_params(("parallel",), 48),
        name="proj",
    )(x2d, g_pre, w_all, w_merge, wuq, g_q, wk, wv, g_kv, place, qtab, ktab, ropetab)


def _compress_kernel(kx_ref, vx_ref, w1k_ref, posk_ref, b1k_ref, w2k_ref, w1v_ref, posv_ref, b1v_ref, w2v_ref,
                     ktab_ref, kc_out, vc_out):
    n_sb = kx_ref.shape[1]
    half = w1k_ref.shape[0] // 2
    per = SEL_BLOCK // CMP_STRIDE

    def one(x_ref, w1_ref, pos_ref, b1_ref, w2_ref):
        x = x_ref[0].astype(F32)
        tops, bots = [], []
        for c in range(per):
            xc = x[:, c * half:(c + 1) * half]
            tops.append(_dot((xc + pos_ref[:, 0:half]).astype(BF16), w1_ref[0:half, :]))
            bots.append(_dot((xc + pos_ref[:, half:2 * half]).astype(BF16), w1_ref[half:2 * half, :]))
        nxt = pltpu.roll(bots[0], n_sb - 1, 0)
        row = lax.broadcasted_iota(jnp.int32, nxt.shape, 0)
        nxt = jnp.where(row == n_sb - 1, 0.0, nxt)
        pre = jnp.concatenate([tops[c] + (bots[c + 1] if c + 1 < per else nxt) for c in range(per)], axis=0)
        return _dot(_gelu_tanh(pre + b1_ref[...]).astype(BF16), w2_ref[...])

    kc_out[0] = (one(kx_ref, w1k_ref, posk_ref, b1k_ref, w2k_ref) + ktab_ref[...]).astype(BF16)
    vc_out[0] = one(vx_ref, w1v_ref, posv_ref, b1v_ref, w2v_ref).astype(BF16)


def _compress_call(kx, vx, w1k, posk, b1k, w2k, w1v, posv, b1v, w2v, ktab):
    b, n_sb, width = kx.shape
    n_rows = n_sb * (SEL_BLOCK // CMP_STRIDE)
    x_spec = pl.BlockSpec((1, n_sb, width), lambda i: (i, 0, 0))
    o_spec = pl.BlockSpec((1, n_rows, NSA_KV_GROUPS * LANES), lambda i: (i, 0, 0))
    consts = (w1k, posk, b1k, w2k, w1v, posv, b1v, w2v, ktab)
    return pl.pallas_call(
        _compress_kernel,
        grid=(b,),
        in_specs=[x_spec, x_spec] + [_const_spec(c.shape) for c in consts],
        out_specs=[o_spec, o_spec],
        out_shape=[jax.ShapeDtypeStruct((b, n_rows, NSA_KV_GROUPS * LANES), BF16)] * 2,
        compiler_params=_vmem_params(("parallel",), 32),
        name="compress",
    )(kx, vx, *consts)


def _normalize(acc):
    return acc * (1.0 / acc[:, ONES_LANE:ONES_LANE + 1])


def _dense_attend(q, tiles, k_at, v_at, s_ref):
    maxes = []
    off = 0
    for start, width, mask in tiles:
        s = _dot_nt(q, k_at(start, width))
        if mask is not None:
            s = jnp.where(mask, s, NEG)
        s_ref[:, off:off + width] = s
        maxes.append(jnp.max(s, axis=1, keepdims=True))
        off += width
    m = functools.reduce(jnp.maximum, maxes)
    acc = None
    off = 0
    for start, width, _ in tiles:
        part = _dot(jnp.exp(s_ref[:, off:off + width] - m).astype(BF16), v_at(start, width))
        acc = part if acc is None else acc + part
        off += width
    return _normalize(acc)


def _split_tiles(start, total, width):
    return [(start + o, min(width, total - o), None) for o in range(0, total, width)]


def _nsa_kernel(q_ref, kcmp_ref, vcmp_ref, ks_ref, vs_ref, kw_ref, vw_ref, gs_ref, o_ref, qsel_ref, ocmp_ref,
                osel_ref, owin_ref, s_ref):
    tq = q_ref.shape[1]
    seq = ks_ref.shape[1]
    qi = pl.program_id(1)
    t0 = qi * tq
    m_rows = NSA_REP * tq
    n_cmp_rows = kcmp_ref.shape[1]
    n_sb = n_cmp_rows // (SEL_BLOCK // CMP_STRIDE)
    per = SEL_BLOCK // CMP_STRIDE
    win_tiles = WINDOW // tq
    groups = [slice(g * LANES, (g + 1) * LANES) for g in range(NSA_KV_GROUPS)]

    def q_group(g):
        return jnp.concatenate(
            [q_ref[0, :, (g * NSA_REP + r) * LANES:(g * NSA_REP + r + 1) * LANES] for r in range(NSA_REP)], axis=0)

    def rel(width):
        return (lax.broadcasted_iota(jnp.int32, (m_rows, width), 1)
                - lax.broadcasted_iota(jnp.int32, (m_rows, width), 0) % tq)

    lane_q = lax.broadcasted_iota(jnp.int32, (m_rows, LANES), 1)
    for g, gl in enumerate(groups):
        qg = q_group(g)

        s_t = _dot_nt(kcmp_ref[0, :, gl], qg)
        krow = lax.broadcasted_iota(jnp.int32, s_t.shape, 0)
        blk_end = (per * (krow % n_sb) + krow // n_sb) * CMP_STRIDE + (CMP_BLOCK - 1)
        t_abs = t0 + lax.broadcasted_iota(jnp.int32, s_t.shape, 1) % tq
        cmask = t_abs >= blk_end
        s_t = jnp.where(cmask, s_t, NEG)
        e = jnp.exp(s_t - jnp.max(s_t, axis=0, keepdims=True))
        p_t = jnp.where(cmask, e, 0.0) * (1.0 / jnp.sum(e, axis=0, keepdims=True))
        o_cmp = _dot_tn(p_t.astype(BF16), vcmp_ref[0, :, gl])

        imp = None
        for r in range(NSA_REP):
            for c in range(per):
                piece = p_t[c * n_sb:(c + 1) * n_sb, r * tq:(r + 1) * tq]
                imp = piece if imp is None else imp + piece
        sb = lax.broadcasted_iota(jnp.int32, (n_sb, tq), 0)
        cur = (t0 + lax.broadcasted_iota(jnp.int32, (n_sb, tq), 1)) // SEL_BLOCK
        forced = (sb == 0) | (sb == cur) | (sb == cur - 1)
        score = jnp.where(forced, BIG, jnp.where(sb > cur, -BIG, imp))
        rank = jnp.zeros((n_sb, tq), jnp.int32)
        for j in range(n_sb):
            sj = score[j:j + 1, :]
            before = (sj > score) | ((sj == score) & (sb > j))
            rank = rank + before.astype(jnp.int32)
        sel_neg = jnp.where(rank < min(N_SEL, n_sb), 0.0, NEG)
        sel_q = jnp.concatenate([jnp.zeros((SEL_LANE0, tq), F32), sel_neg], axis=0).T.astype(BF16)
        qsel_ref[g] = jnp.where(lane_q >= SEL_LANE0, jnp.concatenate([sel_q] * NSA_REP, axis=0), qg)
        ocmp_ref[g] = o_cmp

    def kv_at(k_ref, gl):
        return lambda start, width: k_ref[0, pl.ds(start, width), gl]

    slab = NSA_SEL_SLAB
    for n in range(seq // slab):

        @pl.when(qi // (slab // tq) == n)
        def _(n=n):
            last = rel(slab) <= t0 - n * slab
            tiles = _split_tiles(0, n * slab, slab) + [(n * slab, slab, last)]
            for g, gl in enumerate(groups):
                osel_ref[g] = _dense_attend(qsel_ref[g], tiles, kv_at(ks_ref, gl), kv_at(vs_ref, gl), s_ref.at[g])

    for v in range(win_tiles + 1):

        @pl.when((qi == v) if v < win_tiles else (qi >= win_tiles))
        def _(v=v):
            causal = rel(tq) <= 0
            if v < win_tiles:
                tiles = _split_tiles(0, v * tq, WINDOW) + [(v * tq, tq, causal)]
            else:
                first = pl.multiple_of((qi - win_tiles) * tq, tq)
                tiles = [(first, tq, rel(tq) > 0), (pl.multiple_of(first + tq, tq), WINDOW - tq, None),
                         (pl.multiple_of(t0, tq), tq, causal)]
            for g, gl in enumerate(groups):
                owin_ref[g] = _dense_attend(q_group(g), tiles, kv_at(kw_ref, gl), kv_at(vw_ref, gl), s_ref.at[g])

    gates = gs_ref[0].astype(F32)
    outs = []
    for g in range(NSA_KV_GROUPS):
        for r in range(NSA_REP):
            hd = g * NSA_REP + r
            rs = slice(r * tq, (r + 1) * tq)
            o = (gates[:, hd:hd + 1] * ocmp_ref[g, rs, :]
                 + gates[:, NSA_HEADS + hd:NSA_HEADS + hd + 1] * osel_ref[g, rs, :]
                 + gates[:, 2 * NSA_HEADS + hd:2 * NSA_HEADS + hd + 1] * owin_ref[g, rs, :])
            outs.append(o[:, 0:NSA_HEAD_DIM])
    o_ref[0] = jnp.concatenate(outs, axis=1).astype(o_ref.dtype)


def _nsa_call(q, kcmp, vcmp, ks, vs, kw, vw, gs):
    b, t, _ = q.shape
    tq = NSA_TQ
    m_rows = NSA_REP * tq
    branch_out = pltpu.VMEM((NSA_KV_GROUPS, m_rows, LANES), F32)
    kv_spec = pl.BlockSpec((1, t, NSA_KV_GROUPS * LANES), lambda i, j: (i, 0, 0))
    cmp_spec = pl.BlockSpec((1, kcmp.shape[1], NSA_KV_GROUPS * LANES), lambda i, j: (i, 0, 0))
    return pl.pallas_call(
        _nsa_kernel,
        grid=(b, t // tq),
        in_specs=[pl.BlockSpec((1, tq, NSA_HEADS * LANES), lambda i, j: (i, j, 0)), cmp_spec, cmp_spec,
                  kv_spec, kv_spec, kv_spec, kv_spec, pl.BlockSpec((1, tq, LANES), lambda i, j: (i, j, 0))],
        out_specs=pl.BlockSpec((1, tq, NSA_WIDTH), lambda i, j: (i, j, 0)),
        out_shape=jax.ShapeDtypeStruct((b, t, NSA_WIDTH), BF16),
        scratch_shapes=[pltpu.VMEM((NSA_KV_GROUPS, m_rows, LANES), BF16), branch_out, branch_out, branch_out,
                        pltpu.VMEM((NSA_KV_GROUPS, m_rows, t), F32)],
        compiler_params=_vmem_params(("parallel", "arbitrary"), 48),
        name="nsa",
    )(q, kcmp, vcmp, ks, vs, kw, vw, gs)


def _mla_kernel(qlo_ref, qhi_ref, k_ref, v_ref, o_ref, s_ref):
    tq = qlo_ref.shape[2]
    nq = k_ref.shape[2] // tq
    step = pl.program_id(1)
    pairs = o_ref.shape[1]
    per_pair = MLA_HEADS // pairs

    for n in range(nq // 2):

        @pl.when(step == n)
        def _(n=n):
            causal = (lax.broadcasted_iota(jnp.int32, (tq, tq), 1) <= lax.broadcasted_iota(jnp.int32, (tq, tq), 0))

            def pair_body(pr, carry):
                for which, (q_ref, qt) in enumerate(((qlo_ref, n), (qhi_ref, nq - 1 - n))):
                    tiles = _split_tiles(0, qt * tq, MLA_SLAB) + [(qt * tq, tq, causal)]
                    outs = []
                    for e in range(per_pair):
                        hd = per_pair * pr + e
                        o = _dense_attend(
                            q_ref[0, hd], tiles,
                            lambda start, width, hd=hd: k_ref[0, hd, pl.ds(start, width), :],
                            lambda start, width, hd=hd: v_ref[0, hd, pl.ds(start, width), :],
                            s_ref.at[which * per_pair + e])
                        outs.append(o[:, 0:V_DIM])
                    o_ref[0, pr, qt * tq:(qt + 1) * tq, :] = jnp.concatenate(outs, axis=1).astype(o_ref.dtype)
                return carry

            lax.fori_loop(0, pairs, pair_body, 0)


def _mla_call(q, k, v):
    b, heads, t, width = q.shape
    tq = MLA_TQ
    nq = t // tq
    pairs = heads * V_DIM // LANES
    kv_spec = pl.BlockSpec((1, heads, t, width), lambda i, j: (i, 0, 0, 0))
    return pl.pallas_call(
        _mla_kernel,
        grid=(b, nq // 2),
        in_specs=[pl.BlockSpec((1, heads, tq, width), lambda i, j: (i, 0, j, 0)),
                  pl.BlockSpec((1, heads, tq, width), lambda i, j: (i, 0, nq - 1 - j, 0)), kv_spec, kv_spec],
        out_specs=pl.BlockSpec((1, pairs, t, LANES), lambda i, j: (i, 0, 0, 0)),
        out_shape=jax.ShapeDtypeStruct((b, pairs, t, LANES), BF16),
        scratch_shapes=[pltpu.VMEM((2 * (heads // pairs), tq, t), F32)],
        compiler_params=_vmem_params(("parallel", "arbitrary"), 56),
        name="mla",
    )(q, q, k, v)


def _channel_kernel(on_ref, om_ref, gm_ref, x_ref, p_ref, wbn_ref, wbm_ref, wo_ref, gpost_ref, gpre_ref, wup_ref,
                    wconv_ref, bconv_ref, wdown_ref, g_ref, wple_ref, wpg_ref, gple_ref, out_ref, halo_ref, sa_ref,
                    sv_ref, gate_ref, x1_ref, h_ref, *, tiles_per_seq):
    rows = x_ref.shape[0]
    pad = SUBLANES
    keep = CONV_WIDTH - 1
    seq_start = pl.program_id(0) % tiles_per_seq == 0

    @pl.when(pl.program_id(0) == 0)
    def _():
        halo_ref[...] = jnp.zeros_like(halo_ref)

    a = _dot(on_ref[...], wbn_ref[...])
    b = _dot(om_ref[0, 0], wbm_ref[0])
    for pr in range(1, om_ref.shape[1]):
        b = b + _dot(om_ref[0, pr], wbm_ref[pr])
    merged = gm_ref[:, 0:D_MODEL].astype(F32) * a + gm_ref[:, D_MODEL:2 * D_MODEL].astype(F32) * b
    x1 = x_ref[...] + _rms(_dot(merged.astype(BF16), wo_ref[...]), gpost_ref[...])
    x1_ref[...] = x1
    h_ref[...] = _rms(x1, gpre_ref[...]).astype(BF16)
    h = h_ref[...]

    def conv_chunk(off, s_ref):
        cols = slice(off, off + FFN_CHUNK)
        u = _dot(h, wup_ref[:, cols])
        prev = jnp.where(seq_start, 0.0, halo_ref[pad - keep:pad, cols])
        s_ref[pad - keep:pad, :] = prev
        s_ref[pad:pad + rows, :] = u
        halo_ref[pad - keep:pad, cols] = s_ref[pad + rows - keep:pad + rows, :]
        y = wconv_ref[CONV_WIDTH - 1:CONV_WIDTH, cols] * u + bconv_ref[:, cols]
        for k in range(CONV_WIDTH - 1):
            shift = CONV_WIDTH - 1 - k
            y = y + wconv_ref[k:k + 1, cols] * s_ref[pad - shift:pad - shift + rows, :]
        return y

    n_chunks = D_FF // FFN_CHUNK
    acc = None
    for c0 in range(0, n_chunks, FFN_GROUP):
        chunks = range(c0, min(c0 + FFN_GROUP, n_chunks))
        slot = (c0 // FFN_GROUP) % 2
        for j, c in enumerate(chunks):
            a = conv_chunk(c * FFN_CHUNK, sa_ref.at[c % 2])
            v = conv_chunk(D_FF + c * FFN_CHUNK, sv_ref.at[c % 2])
            gate_ref[slot, :, j * FFN_CHUNK:(j + 1) * FFN_CHUNK] = (_gelu_tanh(a) * v).astype(BF16)
        width = len(chunks) * FFN_CHUNK
        part = _dot(gate_ref[slot, :, 0:width], wdown_ref[c0 * FFN_CHUNK:c0 * FFN_CHUNK + width, :])
        acc = part if acc is None else acc + part
    x2 = x1_ref[...] + _rms(acc, g_ref[...])
    e = _dot(p_ref[...].astype(BF16), wple_ref[...])
    gate = jax.nn.sigmoid(_dot(x2.astype(BF16), wpg_ref[...]))
    out_ref[...] = x2 + _rms(e * gate, gple_ref[...])


def _channel_call(o_nsa, o_mla, gm, x2d, p2d, wbn, wbm, wo, g_post, g_pre_ffn, wup, wconv, bconv, wdown, g_post_ffn,
                  wple, wgate, g_ple):
    n = x2d.shape[0]
    rows = FFN_ROWS
    _, pairs, seq, _ = o_mla.shape
    per_seq = seq // rows

    def row_spec(width=D_MODEL):
        return pl.BlockSpec((rows, width), lambda i: (i, 0))

    mla_spec = pl.BlockSpec((1, pairs, rows, LANES), lambda i: (i // per_seq, 0, i % per_seq, 0))
    consts = (wbn, wbm, wo, g_post, g_pre_ffn, wup, wconv, bconv, wdown, g_post_ffn, wple, wgate, g_ple)
    return pl.pallas_call(
        functools.partial(_channel_kernel, tiles_per_seq=per_seq),
        grid=(n // rows,),
        in_specs=[row_spec(NSA_WIDTH), mla_spec, row_spec(2 * D_MODEL), row_spec(), row_spec(PLE_DIM)]
        + [_const_spec(c.shape) for c in consts],
        out_specs=row_spec(),
        out_shape=jax.ShapeDtypeStruct((n, D_MODEL), F32),
        scratch_shapes=[pltpu.VMEM((SUBLANES, 2 * D_FF), F32),
                        pltpu.VMEM((2, rows + SUBLANES, FFN_CHUNK), F32),
                        pltpu.VMEM((2, rows + SUBLANES, FFN_CHUNK), F32),
                        pltpu.VMEM((2, rows, FFN_GROUP * FFN_CHUNK), BF16),
                        pltpu.VMEM((rows, D_MODEL), F32),
                        pltpu.VMEM((rows, D_MODEL), BF16)],
        compiler_params=_vmem_params(("arbitrary",), 56),
        name="channel",
    )(o_nsa, o_mla, gm, x2d, p2d, *consts)


def _pad_heads(w, heads, width):
    rows = w.shape[0]
    w = w.reshape(rows, heads, width)
    return jnp.pad(w, ((0, 0), (0, 0), (0, LANES - width))).reshape(rows, heads * LANES)


def _swap_halves(w):
    half = w.shape[-1] // 2
    return jnp.concatenate([w[..., half:], w[..., :half]], axis=-1)


def _layout_w_in(w_in):
    sizes = (NSA_WIDTH, 6 * NSA_KV_GROUPS * NSA_HEAD_DIM, 3 * NSA_HEADS, Q_LORA, KV_LORA, QK_ROPE, 2 * D_MODEL)
    offs = np.cumsum((0,) + sizes)
    wq, wkv, wgate, wcq, wckv, wkr, wmerge = (w_in[:, offs[i]:offs[i + 1]] for i in range(len(sizes)))
    rows = w_in.shape[0]
    small = jnp.concatenate([wgate, jnp.zeros((rows, _SMALL_KR - wgate.shape[1]), w_in.dtype), wkr, _swap_halves(wkr),
                             jnp.zeros((rows, LANES - _SMALL_KR - 2 * QK_ROPE), w_in.dtype)], axis=1)
    parts = [wq * (NSA_HEAD_DIM ** -0.5), wkv, small, wcq, wckv]
    w_all = jnp.concatenate([part.astype(BF16) for part in parts], axis=1)
    assert w_all.shape[1] == _C_END
    return w_all, wmerge.astype(BF16)


def _layout_compress(w1, pos, b1, w2):
    dh, g, half_tok = NSA_HEAD_DIM, NSA_KV_GROUPS, CMP_BLOCK // 2
    eye = jnp.eye(g, dtype=w1.dtype)
    w1 = w1.reshape(CMP_BLOCK, dh, dh)
    w1e = jnp.einsum("lde,gh->lgdhe", w1, eye).reshape(CMP_BLOCK * g * dh, g * dh)
    pose = jnp.broadcast_to(pos[:, None, :], (CMP_BLOCK, g, dh)).reshape(1, CMP_BLOCK * g * dh)
    b1e = jnp.tile(b1, g).reshape(1, g * dh)
    w2e = jnp.einsum("de,gh->gdhe", w2, eye)
    w2e = jnp.pad(w2e, ((0, 0), (0, 0), (0, 0), (0, LANES - dh))).reshape(g * dh, g * LANES)
    del half_tok
    return w1e.astype(BF16), pose.astype(F32), b1e.astype(F32), w2e.astype(BF16)


def _position_tables(seq):
    t = np.arange(seq)
    hi = (t // SEL_BLOCK) * SEL_BLOCK
    lo = t % SEL_BLOCK
    qtab = np.zeros((seq, LANES), np.float32)
    qtab[:, FEAT + 0] = -hi
    qtab[:, FEAT + 1] = -lo
    qtab[:, FEAT + 2] = 1.0
    qtab[:, FEAT + 3] = 1.0
    ktab = np.zeros((seq, LANES), np.float32)
    ktab[:, FEAT + 0] = 1.0
    ktab[:, FEAT + 1] = 1.0
    ktab[:, FEAT + 2] = hi
    ktab[:, FEAT + 3] = lo
    ktab[t, SEL_LANE0 + t // SEL_BLOCK] = 1.0
    per = SEL_BLOCK // CMP_STRIDE
    n_sb = seq // SEL_BLOCK
    r = np.arange(per * n_sb)
    end = (per * (r % n_sb) + r // n_sb) * CMP_STRIDE + CMP_BLOCK - 1
    ctab = np.zeros((per * n_sb, NSA_KV_GROUPS, LANES), np.float32)
    ctab[:, :, FEAT + 0] = 1.0
    ctab[:, :, FEAT + 1] = 1.0
    ctab[:, :, FEAT + 2] = ((end // SEL_BLOCK) * SEL_BLOCK)[:, None]
    ctab[:, :, FEAT + 3] = (end % SEL_BLOCK)[:, None]
    return jnp.asarray(qtab), jnp.asarray(ktab), jnp.asarray(ctab.reshape(per * n_sb, NSA_KV_GROUPS * LANES))


def _rope_table(positions):
    inv_freq = ROPE_THETA ** (-jnp.arange(0, QK_ROPE, 2, dtype=F32) / QK_ROPE)
    ang = positions.astype(F32)[..., None] * inv_freq
    cos, sin = jnp.cos(ang), jnp.sin(ang)
    c2 = jnp.concatenate([cos, cos], axis=-1)
    s2 = jnp.concatenate([-sin, sin], axis=-1)
    scale = (QK_NOPE + QK_ROPE) ** -0.5
    lead = c2.shape[:-1]
    mq = jnp.concatenate([jnp.full(lead + (QK_NOPE,), scale, F32), scale * c2, scale * s2], axis=-1)
    rk = jnp.concatenate([jnp.zeros(lead + (_SMALL_KR,), F32), c2, s2,
                          jnp.zeros(lead + (LANES - _SMALL_KR - 2 * QK_ROPE,), F32)], axis=-1)
    return jnp.concatenate([mq, rk], axis=-1).reshape(-1, 2 * LANES)


def _rope_placement():
    place = np.zeros((LANES, MLA_HEADS, LANES), np.float32)
    i = np.arange(QK_ROPE)
    place[_SMALL_KR + i, :, QK_NOPE + i] = 1.0
    place[_SMALL_KR + i, :, QK_NOPE + QK_ROPE + i] = 1.0
    return jnp.asarray(place.reshape(LANES, MLA_HEADS * LANES), dtype=BF16)


def kernel(x, p, positions, g_pre_mix, w_in, nsa_pos_k, nsa_pos_v, nsa_ck_w1, nsa_ck_b1, nsa_ck_w2, nsa_cv_w1, nsa_cv_b1, nsa_cv_w2, mla_g_q, mla_w_uq, mla_g_kv, mla_w_ukv, w_br_nsa, w_br_mla, w_o, g_post_mix, g_pre_ffn, w_up, w_conv, b_conv, w_down, g_post_ffn, w_ple, w_ple_gate, g_ple):
    b, t, d = x.shape
    depth = w_in.shape[0]
    assert d == D_MODEL and t // SEL_BLOCK == LANES - SEL_LANE0 and t % PROJ_ROWS == 0
    n = b * t
    qtab, ktab, ctab = _position_tables(t)
    ropetab = _rope_table(positions)
    place = _rope_placement()
    xc = x.reshape(n, d)
    for i in range(depth):
        w_all, w_merge = _layout_w_in(w_in[i])
        wuq = mla_w_uq[i].reshape(Q_LORA, MLA_HEADS, QK_NOPE + QK_ROPE)
        wuq = jnp.concatenate([wuq, _swap_halves(wuq[..., QK_NOPE:])], axis=-1).reshape(Q_LORA, MLA_HEADS * LANES)
        wukv = mla_w_ukv[i].reshape(KV_LORA, MLA_HEADS, QK_NOPE + V_DIM)
        wk = _pad_heads(wukv[..., :QK_NOPE].reshape(KV_LORA, -1), MLA_HEADS, QK_NOPE)
        wv = _pad_heads(wukv[..., QK_NOPE:].reshape(KV_LORA, -1), MLA_HEADS, V_DIM)
        (q_n, kc, vc, ks, vs, kw, vw, gs, q_m, k_m, v_m, gm) = _proj_call(
            xc, g_pre_mix[i][None], w_all, w_merge, wuq.astype(BF16), mla_g_q[i][None], wk.astype(BF16), wv.astype(BF16),
            mla_g_kv[i][None], place, qtab, ktab, ropetab, t)

        n_sb = t // SEL_BLOCK
        kcmp, vcmp = _compress_call(
            kc.reshape(b, n_sb, SEL_BLOCK * LANES), vc.reshape(b, n_sb, SEL_BLOCK * LANES),
            *_layout_compress(nsa_ck_w1[i], nsa_pos_k[i], nsa_ck_b1[i], nsa_ck_w2[i]),
            *_layout_compress(nsa_cv_w1[i], nsa_pos_v[i], nsa_cv_b1[i], nsa_cv_w2[i]), ctab)

        def seq3(a):
            return a.reshape(b, t, a.shape[-1])

        o_nsa = _nsa_call(seq3(q_n), kcmp, vcmp, seq3(ks), seq3(vs), seq3(kw), seq3(vw), seq3(gs))
        o_mla = _mla_call(q_m, k_m, v_m)

        xc = _channel_call(o_nsa.reshape(n, NSA_WIDTH), o_mla, gm, xc, p[i].reshape(n, PLE_DIM),
                           w_br_nsa[i].astype(BF16), w_br_mla[i].astype(BF16).reshape(-1, LANES, D_MODEL),
                           w_o[i].astype(BF16), g_post_mix[i][None], g_pre_ffn[i][None], w_up[i].astype(BF16),
                           w_conv[i], b_conv[i][None], w_down[i].astype(BF16), g_post_ffn[i][None],
                           w_ple[i].astype(BF16), w_ple_gate[i].astype(BF16), g_ple[i][None])
    return xc.reshape(b, t, d)
```

```python
import functools
import math

import numpy as np
import jax
import jax.numpy as jnp
from jax import lax
from jax.experimental import pallas as pl
from jax.experimental.pallas import tpu as pltpu

F32 = jnp.float32
BF16 = jnp.bfloat16

D_MODEL = 1024
PLE_DIM = 256
NSA_HEADS = 8
NSA_KV_GROUPS = 2
NSA_REP = NSA_HEADS // NSA_KV_GROUPS
NSA_HEAD_DIM = 64
NSA_WIDTH = NSA_HEADS * NSA_HEAD_DIM
CMP_BLOCK = 32
CMP_STRIDE = 16
SEL_BLOCK = 64
N_SEL = 16
WINDOW = 512
MLA_HEADS = 8
Q_LORA = 256
KV_LORA = 128
QK_NOPE = 64
QK_ROPE = 32
V_DIM = 64
MLA_WIDTH = MLA_HEADS * V_DIM
ROPE_THETA = 10000.0
D_FF = 2816
CONV_WIDTH = 3
EPS = 1e-6
NEG = -1e30
BIG = 1e9
ALIBI_SLOPES = tuple(2.0 ** (-8.0 * (i + 1) / NSA_HEADS) for i in range(NSA_HEADS))

LANES = 128
SUBLANES = 8

FEAT = NSA_HEAD_DIM
SEL_LANE0 = 96
ONES_LANE = NSA_HEAD_DIM

PROJ_ROWS = 512
NSA_TQ = 256
NSA_SEL_SLAB = 512
MLA_TQ = 256
MLA_SLAB = 512
FFN_ROWS = 512
FFN_CHUNK = 256
FFN_GROUP = 4

_C_Q = 0
_C_KC = _C_Q + NSA_WIDTH
_C_VC = _C_KC + LANES
_C_KS = _C_VC + LANES
_C_VS = _C_KS + LANES
_C_KW = _C_VS + LANES
_C_VW = _C_KW + LANES
_C_SMALL = _C_VW + LANES
_C_CQ = _C_SMALL + LANES
_C_CKV = _C_CQ + Q_LORA
_C_MERGE = _C_CKV + KV_LORA
_C_END = _C_MERGE + 2 * D_MODEL
_SMALL_KR = 32


def _vmem_params(semantics, mib):
    return pltpu.CompilerParams(dimension_semantics=semantics, vmem_limit_bytes=mib * 1024 * 1024)


def _const_spec(shape):
    nd = len(shape)
    return pl.BlockSpec(shape, lambda *_: (0,) * nd, pipeline_mode=pl.Buffered(1))


def _rms(x, g):
    return x * lax.rsqrt(jnp.mean(x * x, axis=-1, keepdims=True) + EPS) * g


def _gelu_tanh(x):
    return 0.5 * x * (1.0 + jnp.tanh(math.sqrt(2.0 / math.pi) * (x + 0.044715 * (x * x * x))))


def _dot(a, b):
    return jnp.dot(a, b, preferred_element_type=F32)


def _dot_nt(a, b):
    return lax.dot_general(a, b, (((1,), (1,)), ((), ())), preferred_element_type=F32)


def _dot_tn(a, b):
    return lax.dot_general(a, b, (((0,), (0,)), ((), ())), preferred_element_type=F32)


def _proj_kernel(x_ref, g_ref, w_ref, wuq_ref, gq_ref, wk_ref, wv_ref, gkv_ref, place_ref, qtab_ref, ktab_ref,
                 rope_ref, q_out, kc_out, vc_out, ks_out, vs_out, kw_out, vw_out, gs_out, qm_out, km_out,
                 vm_out, gm_out):
    rows = x_ref.shape[0]
    h = _rms(x_ref[...], g_ref[...]).astype(BF16)

    def proj(a, b):
        return _dot(h, w_ref[:, a:b])

    lane = lax.broadcasted_iota(jnp.int32, (rows, LANES), 1)
    ones_col = (lane == ONES_LANE).astype(F32)

    data = lane < NSA_HEAD_DIM

    def halves(col):
        packed = proj(col, col + LANES)
        return packed, pltpu.roll(packed, LANES - NSA_HEAD_DIM, 1)

    qtab = qtab_ref[...]
    for pr in range(NSA_HEADS // 2):
        for hd, qh in zip((2 * pr, 2 * pr + 1), halves(_C_Q + pr * LANES)):
            q_out[:, hd * LANES:(hd + 1) * LANES] = jnp.where(data, qh, ALIBI_SLOPES[hd] * qtab).astype(BF16)

    kc_out[...] = proj(_C_KC, _C_KC + LANES).astype(BF16)
    vc_out[...] = proj(_C_VC, _C_VC + LANES).astype(BF16)

    ktab_sel = ktab_ref[...]
    ktab_win = jnp.where(lane < SEL_LANE0, ktab_sel, 0.0)
    for col, out, feat in ((_C_KS, ks_out, ktab_sel), (_C_VS, vs_out, ones_col), (_C_KW, kw_out, ktab_win),
                           (_C_VW, vw_out, ones_col)):
        for g, part in enumerate(halves(col)):
            out[:, g * LANES:(g + 1) * LANES] = jnp.where(data, part, feat).astype(BF16)

    small = proj(_C_SMALL, _C_SMALL + LANES)
    gs_out[...] = jax.nn.sigmoid(small).astype(BF16)

    mq = rope_ref[:, 0:LANES]
    rk = rope_ref[:, LANES:2 * LANES]
    t = small * rk
    k_rope = (t + pltpu.roll(t, LANES - QK_ROPE, 1)).astype(BF16)

    cqn = _rms(proj(_C_CQ, _C_CQ + Q_LORA), gq_ref[...]).astype(BF16)
    for hd in range(MLA_HEADS):
        sl = slice(hd * LANES, (hd + 1) * LANES)
        qm_out[0, hd] = (_dot(cqn, wuq_ref[:, sl]) * mq).astype(BF16)

    ckvn = _rms(proj(_C_CKV, _C_CKV + KV_LORA), gkv_ref[...]).astype(BF16)
    for hd in range(MLA_HEADS):
        sl = slice(hd * LANES, (hd + 1) * LANES)
        km_out[0, hd] = (_dot(ckvn, wk_ref[:, sl]) + _dot(k_rope, place_ref[:, sl])).astype(BF16)
        vm_out[0, hd] = (_dot(ckvn, wv_ref[:, sl]) + ones_col).astype(BF16)

    half = D_MODEL // 2
    for c in range(2 * D_MODEL // half):
        a = _C_MERGE + c * half
        gm_out[:, c * half:(c + 1) * half] = jax.nn.sigmoid(proj(a, a + half)).astype(BF16)


def _proj_call(x2d, g_pre, w_all, wuq, g_q, wk, wv, g_kv, place, qtab, ktab, ropetab, seq):
    n = x2d.shape[0]
    rows = PROJ_ROWS
    per_seq = seq // rows

    def row_spec(width):
        return pl.BlockSpec((rows, width), lambda i: (i, 0))

    tab_spec = pl.BlockSpec((rows, LANES), lambda i: (i % per_seq, 0))
    head_spec = pl.BlockSpec((1, MLA_HEADS, rows, LANES), lambda i: (i // per_seq, 0, i % per_seq, 0))
    head_shape = jax.ShapeDtypeStruct((n // seq, MLA_HEADS, seq, LANES), BF16)
    widths = (NSA_HEADS * LANES, LANES, LANES, NSA_KV_GROUPS * LANES, NSA_KV_GROUPS * LANES,
              NSA_KV_GROUPS * LANES, NSA_KV_GROUPS * LANES, LANES)
    return pl.pallas_call(
        _proj_kernel,
        grid=(n // rows,),
        in_specs=[row_spec(D_MODEL), _const_spec(g_pre.shape), _const_spec(w_all.shape), _const_spec(wuq.shape),
                  _const_spec(g_q.shape), _const_spec(wk.shape), _const_spec(wv.shape), _const_spec(g_kv.shape),
                  _const_spec(place.shape), tab_spec, tab_spec, row_spec(2 * LANES)],
        out_specs=[row_spec(w) for w in widths] + [head_spec] * 3 + [row_spec(2 * D_MODEL)],
        out_shape=[jax.ShapeDtypeStruct((n, w), BF16) for w in widths] + [head_shape] * 3
        + [jax.ShapeDtypeStruct((n, 2 * D_MODEL), BF16)],
        compiler_params=_vmem_params(("parallel",), 48),
        name="proj",
    )(x2d, g_pre, w_all, wuq, g_q, wk, wv, g_kv, place, qtab, ktab, ropetab)


def _compress_kernel(kx_ref, vx_ref, w1k_ref, posk_ref, b1k_ref, w2k_ref, w1v_ref, posv_ref, b1v_ref, w2v_ref,
                     ktab_ref, kc_out, vc_out):
    n_sb = kx_ref.shape[1]
    half = w1k_ref.shape[0] // 2
    per = SEL_BLOCK // CMP_STRIDE

    def one(x_ref, w1_ref, pos_ref, b1_ref, w2_ref):
        x = x_ref[0].astype(F32)
        tops, bots = [], []
        for c in range(per):
            xc = x[:, c * half:(c + 1) * half]
            tops.append(_dot((xc + pos_ref[:, 0:half]).astype(BF16), w1_ref[0:half, :]))
            bots.append(_dot((xc + pos_ref[:, half:2 * half]).astype(BF16), w1_ref[half:2 * half, :]))
        nxt = pltpu.roll(bots[0], n_sb - 1, 0)
        row = lax.broadcasted_iota(jnp.int32, nxt.shape, 0)
        nxt = jnp.where(row == n_sb - 1, 0.0, nxt)
        pre = jnp.concatenate([tops[c] + (bots[c + 1] if c + 1 < per else nxt) for c in range(per)], axis=0)
        return _dot(_gelu_tanh(pre + b1_ref[...]).astype(BF16), w2_ref[...])

    kc_out[0] = (one(kx_ref, w1k_ref, posk_ref, b1k_ref, w2k_ref) + ktab_ref[...]).astype(BF16)
    vc_out[0] = one(vx_ref, w1v_ref, posv_ref, b1v_ref, w2v_ref).astype(BF16)


def _compress_call(kx, vx, w1k, posk, b1k, w2k, w1v, posv, b1v, w2v, ktab):
    b, n_sb, width = kx.shape
    n_rows = n_sb * (SEL_BLOCK // CMP_STRIDE)
    x_spec = pl.BlockSpec((1, n_sb, width), lambda i: (i, 0, 0))
    o_spec = pl.BlockSpec((1, n_rows, NSA_KV_GROUPS * LANES), lambda i: (i, 0, 0))
    consts = (w1k, posk, b1k, w2k, w1v, posv, b1v, w2v, ktab)
    return pl.pallas_call(
        _compress_kernel,
        grid=(b,),
        in_specs=[x_spec, x_spec] + [_const_spec(c.shape) for c in consts],
        out_specs=[o_spec, o_spec],
        out_shape=[jax.ShapeDtypeStruct((b, n_rows, NSA_KV_GROUPS * LANES), BF16)] * 2,
        compiler_params=_vmem_params(("parallel",), 32),
        name="compress",
    )(kx, vx, *consts)


def _normalize(acc):
    return acc * (1.0 / acc[:, ONES_LANE:ONES_LANE + 1])


def _dense_attend(q, tiles, k_at, v_at, s_ref):
    maxes = []
    off = 0
    for start, width, mask in tiles:
        s = _dot_nt(q, k_at(start, width))
        if mask is not None:
            s = jnp.where(mask, s, NEG)
        s_ref[:, off:off + width] = s
        maxes.append(jnp.max(s, axis=1, keepdims=True))
        off += width
    m = functools.reduce(jnp.maximum, maxes)
    acc = None
    off = 0
    for start, width, _ in tiles:
        part = _dot(jnp.exp(s_ref[:, off:off + width] - m).astype(BF16), v_at(start, width))
        acc = part if acc is None else acc + part
        off += width
    return _normalize(acc)


def _split_tiles(start, total, width):
    return [(start + o, min(width, total - o), None) for o in range(0, total, width)]


def _nsa_kernel(q_ref, kcmp_ref, vcmp_ref, ks_ref, vs_ref, kw_ref, vw_ref, gs_ref, o_ref, qsel_ref, ocmp_ref,
                osel_ref, owin_ref, s_ref):
    tq = q_ref.shape[1]
    seq = ks_ref.shape[1]
    qi = pl.program_id(1)
    t0 = qi * tq
    m_rows = NSA_REP * tq
    n_cmp_rows = kcmp_ref.shape[1]
    n_sb = n_cmp_rows // (SEL_BLOCK // CMP_STRIDE)
    per = SEL_BLOCK // CMP_STRIDE
    win_tiles = WINDOW // tq
    groups = [slice(g * LANES, (g + 1) * LANES) for g in range(NSA_KV_GROUPS)]

    def q_group(g):
        return jnp.concatenate(
            [q_ref[0, :, (g * NSA_REP + r) * LANES:(g * NSA_REP + r + 1) * LANES] for r in range(NSA_REP)], axis=0)

    def rel(width):
        return (lax.broadcasted_iota(jnp.int32, (m_rows, width), 1)
                - lax.broadcasted_iota(jnp.int32, (m_rows, width), 0) % tq)

    lane_q = lax.broadcasted_iota(jnp.int32, (m_rows, LANES), 1)
    for g, gl in enumerate(groups):
        qg = q_group(g)

        s_t = _dot_nt(kcmp_ref[0, :, gl], qg)
        krow = lax.broadcasted_iota(jnp.int32, s_t.shape, 0)
        blk_end = (per * (krow % n_sb) + krow // n_sb) * CMP_STRIDE + (CMP_BLOCK - 1)
        t_abs = t0 + lax.broadcasted_iota(jnp.int32, s_t.shape, 1) % tq
        cmask = t_abs >= blk_end
        s_t = jnp.where(cmask, s_t, NEG)
        e = jnp.exp(s_t - jnp.max(s_t, axis=0, keepdims=True))
        p_t = jnp.where(cmask, e, 0.0) * (1.0 / jnp.sum(e, axis=0, keepdims=True))
        o_cmp = _dot_tn(p_t.astype(BF16), vcmp_ref[0, :, gl])

        imp = None
        for r in range(NSA_REP):
            for c in range(per):
                piece = p_t[c * n_sb:(c + 1) * n_sb, r * tq:(r + 1) * tq]
                imp = piece if imp is None else imp + piece
        sb = lax.broadcasted_iota(jnp.int32, (n_sb, tq), 0)
        cur = (t0 + lax.broadcasted_iota(jnp.int32, (n_sb, tq), 1)) // SEL_BLOCK
        forced = (sb == 0) | (sb == cur) | (sb == cur - 1)
        score = jnp.where(forced, BIG, jnp.where(sb > cur, -BIG, imp))
        rank = jnp.zeros((n_sb, tq), jnp.int32)
        for j in range(n_sb):
            sj = score[j:j + 1, :]
            before = (sj > score) | ((sj == score) & (sb > j))
            rank = rank + before.astype(jnp.int32)
        sel_neg = jnp.where(rank < min(N_SEL, n_sb), 0.0, NEG)
        sel_q = jnp.concatenate([jnp.zeros((SEL_LANE0, tq), F32), sel_neg], axis=0).T.astype(BF16)
        qsel_ref[g] = jnp.where(lane_q >= SEL_LANE0, jnp.concatenate([sel_q] * NSA_REP, axis=0), qg)
        ocmp_ref[g] = o_cmp

    def kv_at(k_ref, gl):
        return lambda start, width: k_ref[0, pl.ds(start, width), gl]

    slab = NSA_SEL_SLAB
    for n in range(seq // slab):

        @pl.when(qi // (slab // tq) == n)
        def _(n=n):
            last = rel(slab) <= t0 - n * slab
            tiles = _split_tiles(0, n * slab, slab) + [(n * slab, slab, last)]
            for g, gl in enumerate(groups):
                osel_ref[g] = _dense_attend(qsel_ref[g], tiles, kv_at(ks_ref, gl), kv_at(vs_ref, gl), s_ref.at[g])

    for v in range(win_tiles + 1):

        @pl.when((qi == v) if v < win_tiles else (qi >= win_tiles))
        def _(v=v):
            causal = rel(tq) <= 0
            if v < win_tiles:
                tiles = _split_tiles(0, v * tq, WINDOW) + [(v * tq, tq, causal)]
            else:
                first = pl.multiple_of((qi - win_tiles) * tq, tq)
                tiles = [(first, tq, rel(tq) > 0), (pl.multiple_of(first + tq, tq), WINDOW - tq, None),
                         (pl.multiple_of(t0, tq), tq, causal)]
            for g, gl in enumerate(groups):
                owin_ref[g] = _dense_attend(q_group(g), tiles, kv_at(kw_ref, gl), kv_at(vw_ref, gl), s_ref.at[g])

    gates = gs_ref[0].astype(F32)
    outs = []
    for g in range(NSA_KV_GROUPS):
        for r in range(NSA_REP):
            hd = g * NSA_REP + r
            rs = slice(r * tq, (r + 1) * tq)
            o = (gates[:, hd:hd + 1] * ocmp_ref[g, rs, :]
                 + gates[:, NSA_HEADS + hd:NSA_HEADS + hd + 1] * osel_ref[g, rs, :]
                 + gates[:, 2 * NSA_HEADS + hd:2 * NSA_HEADS + hd + 1] * owin_ref[g, rs, :])
            outs.append(o[:, 0:NSA_HEAD_DIM])
    o_ref[0] = jnp.concatenate(outs, axis=1).astype(o_ref.dtype)


def _nsa_call(q, kcmp, vcmp, ks, vs, kw, vw, gs):
    b, t, _ = q.shape
    tq = NSA_TQ
    m_rows = NSA_REP * tq
    branch_out = pltpu.VMEM((NSA_KV_GROUPS, m_rows, LANES), F32)
    kv_spec = pl.BlockSpec((1, t, NSA_KV_GROUPS * LANES), lambda i, j: (i, 0, 0))
    cmp_spec = pl.BlockSpec((1, kcmp.shape[1], NSA_KV_GROUPS * LANES), lambda i, j: (i, 0, 0))
    return pl.pallas_call(
        _nsa_kernel,
        grid=(b, t // tq),
        in_specs=[pl.BlockSpec((1, tq, NSA_HEADS * LANES), lambda i, j: (i, j, 0)), cmp_spec, cmp_spec,
                  kv_spec, kv_spec, kv_spec, kv_spec, pl.BlockSpec((1, tq, LANES), lambda i, j: (i, j, 0))],
        out_specs=pl.BlockSpec((1, tq, NSA_WIDTH), lambda i, j: (i, j, 0)),
        out_shape=jax.ShapeDtypeStruct((b, t, NSA_WIDTH), BF16),
        scratch_shapes=[pltpu.VMEM((NSA_KV_GROUPS, m_rows, LANES), BF16), branch_out, branch_out, branch_out,
                        pltpu.VMEM((NSA_KV_GROUPS, m_rows, t), F32)],
        compiler_params=_vmem_params(("parallel", "arbitrary"), 48),
        name="nsa",
    )(q, kcmp, vcmp, ks, vs, kw, vw, gs)


def _mla_kernel(qlo_ref, qhi_ref, k_ref, v_ref, o_ref, s_ref):
    tq = qlo_ref.shape[2]
    nq = k_ref.shape[2] // tq
    step = pl.program_id(1)
    pairs = o_ref.shape[1]
    per_pair = MLA_HEADS // pairs

    for n in range(nq // 2):

        @pl.when(step == n)
        def _(n=n):
            causal = (lax.broadcasted_iota(jnp.int32, (tq, tq), 1) <= lax.broadcasted_iota(jnp.int32, (tq, tq), 0))

            def pair_body(pr, carry):
                for which, (q_ref, qt) in enumerate(((qlo_ref, n), (qhi_ref, nq - 1 - n))):
                    tiles = _split_tiles(0, qt * tq, MLA_SLAB) + [(qt * tq, tq, causal)]
                    outs = []
                    for e in range(per_pair):
                        hd = per_pair * pr + e
                        o = _dense_attend(
                            q_ref[0, hd], tiles,
                            lambda start, width, hd=hd: k_ref[0, hd, pl.ds(start, width), :],
                            lambda start, width, hd=hd: v_ref[0, hd, pl.ds(start, width), :],
                            s_ref.at[which * per_pair + e])
                        outs.append(o[:, 0:V_DIM])
                    o_ref[0, pr, qt * tq:(qt + 1) * tq, :] = jnp.concatenate(outs, axis=1).astype(o_ref.dtype)
                return carry

            lax.fori_loop(0, pairs, pair_body, 0)


def _mla_call(q, k, v):
    b, heads, t, width = q.shape
    tq = MLA_TQ
    nq = t // tq
    pairs = heads * V_DIM // LANES
    kv_spec = pl.BlockSpec((1, heads, t, width), lambda i, j: (i, 0, 0, 0))
    return pl.pallas_call(
        _mla_kernel,
        grid=(b, nq // 2),
        in_specs=[pl.BlockSpec((1, heads, tq, width), lambda i, j: (i, 0, j, 0)),
                  pl.BlockSpec((1, heads, tq, width), lambda i, j: (i, 0, nq - 1 - j, 0)), kv_spec, kv_spec],
        out_specs=pl.BlockSpec((1, pairs, t, LANES), lambda i, j: (i, 0, 0, 0)),
        out_shape=jax.ShapeDtypeStruct((b, pairs, t, LANES), BF16),
        scratch_shapes=[pltpu.VMEM((2 * (heads // pairs), tq, t), F32)],
        compiler_params=_vmem_params(("parallel", "arbitrary"), 56),
        name="mla",
    )(q, q, k, v)


def _channel_kernel(on_ref, om_ref, gm_ref, x_ref, p_ref, wbn_ref, wbm_ref, wo_ref, gpost_ref, gpre_ref, wup_ref,
                    wconv_ref, bconv_ref, wdown_ref, g_ref, wple_ref, wpg_ref, gple_ref, out_ref, halo_ref, sa_ref,
                    sv_ref, gate_ref, x1_ref, h_ref, *, tiles_per_seq):
    rows = x_ref.shape[0]
    pad = SUBLANES
    keep = CONV_WIDTH - 1
    seq_start = pl.program_id(0) % tiles_per_seq == 0

    @pl.when(pl.program_id(0) == 0)
    def _():
        halo_ref[...] = jnp.zeros_like(halo_ref)

    a = _dot(on_ref[...], wbn_ref[...])
    b = _dot(om_ref[0, 0], wbm_ref[0])
    for pr in range(1, om_ref.shape[1]):
        b = b + _dot(om_ref[0, pr], wbm_ref[pr])
    merged = gm_ref[:, 0:D_MODEL].astype(F32) * a + gm_ref[:, D_MODEL:2 * D_MODEL].astype(F32) * b
    x1 = x_ref[...] + _rms(_dot(merged.astype(BF16), wo_ref[...]), gpost_ref[...])
    x1_ref[...] = x1
    h_ref[...] = _rms(x1, gpre_ref[...]).astype(BF16)
    h = h_ref[...]

    def conv_chunk(off, s_ref):
        cols = slice(off, off + FFN_CHUNK)
        u = _dot(h, wup_ref[:, cols])
        prev = jnp.where(seq_start, 0.0, halo_ref[pad - keep:pad, cols])
        s_ref[pad - keep:pad, :] = prev
        s_ref[pad:pad + rows, :] = u
        halo_ref[pad - keep:pad, cols] = s_ref[pad + rows - keep:pad + rows, :]
        y = wconv_ref[CONV_WIDTH - 1:CONV_WIDTH, cols] * u + bconv_ref[:, cols]
        for k in range(CONV_WIDTH - 1):
            shift = CONV_WIDTH - 1 - k
            y = y + wconv_ref[k:k + 1, cols] * s_ref[pad - shift:pad - shift + rows, :]
        return y

    n_chunks = D_FF // FFN_CHUNK
    acc = None
    for c0 in range(0, n_chunks, FFN_GROUP):
        chunks = range(c0, min(c0 + FFN_GROUP, n_chunks))
        slot = (c0 // FFN_GROUP) % 2
        for j, c in enumerate(chunks):
            a = conv_chunk(c * FFN_CHUNK, sa_ref.at[c % 2])
            v = conv_chunk(D_FF + c * FFN_CHUNK, sv_ref.at[c % 2])
            gate_ref[slot, :, j * FFN_CHUNK:(j + 1) * FFN_CHUNK] = (_gelu_tanh(a) * v).astype(BF16)
        width = len(chunks) * FFN_CHUNK
        part = _dot(gate_ref[slot, :, 0:width], wdown_ref[c0 * FFN_CHUNK:c0 * FFN_CHUNK + width, :])
        acc = part if acc is None else acc + part
    x2 = x1_ref[...] + _rms(acc, g_ref[...])
    e = _dot(p_ref[...].astype(BF16), wple_ref[...])
    gate = jax.nn.sigmoid(_dot(x2.astype(BF16), wpg_ref[...]))
    out_ref[...] = x2 + _rms(e * gate, gple_ref[...])


def _channel_call(o_nsa, o_mla, gm, x2d, p2d, wbn, wbm, wo, g_post, g_pre_ffn, wup, wconv, bconv, wdown, g_post_ffn,
                  wple, wgate, g_ple):
    n = x2d.shape[0]
    rows = FFN_ROWS
    _, pairs, seq, _ = o_mla.shape
    per_seq = seq // rows

    def row_spec(width=D_MODEL):
        return pl.BlockSpec((rows, width), lambda i: (i, 0))

    mla_spec = pl.BlockSpec((1, pairs, rows, LANES), lambda i: (i // per_seq, 0, i % per_seq, 0))
    consts = (wbn, wbm, wo, g_post, g_pre_ffn, wup, wconv, bconv, wdown, g_post_ffn, wple, wgate, g_ple)
    return pl.pallas_call(
        functools.partial(_channel_kernel, tiles_per_seq=per_seq),
        grid=(n // rows,),
        in_specs=[row_spec(NSA_WIDTH), mla_spec, row_spec(2 * D_MODEL), row_spec(), row_spec(PLE_DIM)]
        + [_const_spec(c.shape) for c in consts],
        out_specs=row_spec(),
        out_shape=jax.ShapeDtypeStruct((n, D_MODEL), F32),
        scratch_shapes=[pltpu.VMEM((SUBLANES, 2 * D_FF), F32),
                        pltpu.VMEM((2, rows + SUBLANES, FFN_CHUNK), F32),
                        pltpu.VMEM((2, rows + SUBLANES, FFN_CHUNK), F32),
                        pltpu.VMEM((2, rows, FFN_GROUP * FFN_CHUNK), BF16),
                        pltpu.VMEM((rows, D_MODEL), F32),
                        pltpu.VMEM((rows, D_MODEL), BF16)],
        compiler_params=_vmem_params(("arbitrary",), 56),
        name="channel",
    )(o_nsa, o_mla, gm, x2d, p2d, *consts)


def _pad_heads(w, heads, width):
    rows = w.shape[0]
    w = w.reshape(rows, heads, width)
    return jnp.pad(w, ((0, 0), (0, 0), (0, LANES - width))).reshape(rows, heads * LANES)


def _swap_halves(w):
    half = w.shape[-1] // 2
    return jnp.concatenate([w[..., half:], w[..., :half]], axis=-1)


def _layout_w_in(w_in):
    sizes = (NSA_WIDTH, 6 * NSA_KV_GROUPS * NSA_HEAD_DIM, 3 * NSA_HEADS, Q_LORA, KV_LORA, QK_ROPE, 2 * D_MODEL)
    offs = np.cumsum((0,) + sizes)
    wq, wkv, wgate, wcq, wckv, wkr, wmerge = (w_in[:, offs[i]:offs[i + 1]] for i in range(len(sizes)))
    rows = w_in.shape[0]
    small = jnp.concatenate([wgate, jnp.zeros((rows, _SMALL_KR - wgate.shape[1]), w_in.dtype), wkr, _swap_halves(wkr),
                             jnp.zeros((rows, LANES - _SMALL_KR - 2 * QK_ROPE), w_in.dtype)], axis=1)
    parts = [wq * (NSA_HEAD_DIM ** -0.5), wkv, small, wcq, wckv, wmerge]
    w_all = jnp.concatenate([part.astype(BF16) for part in parts], axis=1)
    assert w_all.shape[1] == _C_END
    return w_all


def _layout_compress(w1, pos, b1, w2):
    dh, g, half_tok = NSA_HEAD_DIM, NSA_KV_GROUPS, CMP_BLOCK // 2
    eye = jnp.eye(g, dtype=w1.dtype)
    w1 = w1.reshape(CMP_BLOCK, dh, dh)
    w1e = jnp.einsum("lde,gh->lgdhe", w1, eye).reshape(CMP_BLOCK * g * dh, g * dh)
    pose = jnp.broadcast_to(pos[:, None, :], (CMP_BLOCK, g, dh)).reshape(1, CMP_BLOCK * g * dh)
    b1e = jnp.tile(b1, g).reshape(1, g * dh)
    w2e = jnp.einsum("de,gh->gdhe", w2, eye)
    w2e = jnp.pad(w2e, ((0, 0), (0, 0), (0, 0), (0, LANES - dh))).reshape(g * dh, g * LANES)
    del half_tok
    return w1e.astype(BF16), pose.astype(F32), b1e.astype(F32), w2e.astype(BF16)


def _position_tables(seq):
    t = np.arange(seq)
    hi = (t // SEL_BLOCK) * SEL_BLOCK
    lo = t % SEL_BLOCK
    qtab = np.zeros((seq, LANES), np.float32)
    qtab[:, FEAT + 0] = -hi
    qtab[:, FEAT + 1] = -lo
    qtab[:, FEAT + 2] = 1.0
    qtab[:, FEAT + 3] = 1.0
    ktab = np.zeros((seq, LANES), np.float32)
    ktab[:, FEAT + 0] = 1.0
    ktab[:, FEAT + 1] = 1.0
    ktab[:, FEAT + 2] = hi
    ktab[:, FEAT + 3] = lo
    ktab[t, SEL_LANE0 + t // SEL_BLOCK] = 1.0
    per = SEL_BLOCK // CMP_STRIDE
    n_sb = seq // SEL_BLOCK
    r = np.arange(per * n_sb)
    end = (per * (r % n_sb) + r // n_sb) * CMP_STRIDE + CMP_BLOCK - 1
    ctab = np.zeros((per * n_sb, NSA_KV_GROUPS, LANES), np.float32)
    ctab[:, :, FEAT + 0] = 1.0
    ctab[:, :, FEAT + 1] = 1.0
    ctab[:, :, FEAT + 2] = ((end // SEL_BLOCK) * SEL_BLOCK)[:, None]
    ctab[:, :, FEAT + 3] = (end % SEL_BLOCK)[:, None]
    return jnp.asarray(qtab), jnp.asarray(ktab), jnp.asarray(ctab.reshape(per * n_sb, NSA_KV_GROUPS * LANES))


def _rope_table(positions):
    inv_freq = ROPE_THETA ** (-jnp.arange(0, QK_ROPE, 2, dtype=F32) / QK_ROPE)
    ang = positions.astype(F32)[..., None] * inv_freq
    cos, sin = jnp.cos(ang), jnp.sin(ang)
    c2 = jnp.concatenate([cos, cos], axis=-1)
    s2 = jnp.concatenate([-sin, sin], axis=-1)
    scale = (QK_NOPE + QK_ROPE) ** -0.5
    lead = c2.shape[:-1]
    mq = jnp.concatenate([jnp.full(lead + (QK_NOPE,), scale, F32), scale * c2, scale * s2], axis=-1)
    rk = jnp.concatenate([jnp.zeros(lead + (_SMALL_KR,), F32), c2, s2,
                          jnp.zeros(lead + (LANES - _SMALL_KR - 2 * QK_ROPE,), F32)], axis=-1)
    return jnp.concatenate([mq, rk], axis=-1).reshape(-1, 2 * LANES)


def _rope_placement():
    place = np.zeros((LANES, MLA_HEADS, LANES), np.float32)
    i = np.arange(QK_ROPE)
    place[_SMALL_KR + i, :, QK_NOPE + i] = 1.0
    place[_SMALL_KR + i, :, QK_NOPE + QK_ROPE + i] = 1.0
    return jnp.asarray(place.reshape(LANES, MLA_HEADS * LANES), dtype=BF16)


def kernel(x, p, positions, g_pre_mix, w_in, nsa_pos_k, nsa_pos_v, nsa_ck_w1, nsa_ck_b1, nsa_ck_w2, nsa_cv_w1, nsa_cv_b1, nsa_cv_w2, mla_g_q, mla_w_uq, mla_g_kv, mla_w_ukv, w_br_nsa, w_br_mla, w_o, g_post_mix, g_pre_ffn, w_up, w_conv, b_conv, w_down, g_post_ffn, w_ple, w_ple_gate, g_ple):
    b, t, d = x.shape
    depth = w_in.shape[0]
    assert d == D_MODEL and t // SEL_BLOCK == LANES - SEL_LANE0 and t % PROJ_ROWS == 0
    n = b * t
    qtab, ktab, ctab = _position_tables(t)
    ropetab = _rope_table(positions)
    place = _rope_placement()
    xc = x.reshape(n, d)
    for i in range(depth):
        w_all = _layout_w_in(w_in[i])
        wuq = mla_w_uq[i].reshape(Q_LORA, MLA_HEADS, QK_NOPE + QK_ROPE)
        wuq = jnp.concatenate([wuq, _swap_halves(wuq[..., QK_NOPE:])], axis=-1).reshape(Q_LORA, MLA_HEADS * LANES)
        wukv = mla_w_ukv[i].reshape(KV_LORA, MLA_HEADS, QK_NOPE + V_DIM)
        wk = _pad_heads(wukv[..., :QK_NOPE].reshape(KV_LORA, -1), MLA_HEADS, QK_NOPE)
        wv = _pad_heads(wukv[..., QK_NOPE:].reshape(KV_LORA, -1), MLA_HEADS, V_DIM)
        (q_n, kc, vc, ks, vs, kw, vw, gs, q_m, k_m, v_m, gm) = _proj_call(
            xc, g_pre_mix[i][None], w_all, wuq.astype(BF16), mla_g_q[i][None], wk.astype(BF16), wv.astype(BF16),
            mla_g_kv[i][None], place, qtab, ktab, ropetab, t)

        n_sb = t // SEL_BLOCK
        kcmp, vcmp = _compress_call(
            kc.reshape(b, n_sb, SEL_BLOCK * LANES), vc.reshape(b, n_sb, SEL_BLOCK * LANES),
            *_layout_compress(nsa_ck_w1[i], nsa_pos_k[i], nsa_ck_b1[i], nsa_ck_w2[i]),
            *_layout_compress(nsa_cv_w1[i], nsa_pos_v[i], nsa_cv_b1[i], nsa_cv_w2[i]), ctab)

        def seq3(a):
            return a.reshape(b, t, a.shape[-1])

        o_nsa = _nsa_call(seq3(q_n), kcmp, vcmp, seq3(ks), seq3(vs), seq3(kw), seq3(vw), seq3(gs))
        o_mla = _mla_call(q_m, k_m, v_m)

        xc = _channel_call(o_nsa.reshape(n, NSA_WIDTH), o_mla, gm, xc, p[i].reshape(n, PLE_DIM),
                           w_br_nsa[i].astype(BF16), w_br_mla[i].astype(BF16).reshape(-1, LANES, D_MODEL),
                           w_o[i].astype(BF16), g_post_mix[i][None], g_pre_ffn[i][None], w_up[i].astype(BF16),
                           w_conv[i], b_conv[i][None], w_down[i].astype(BF16), g_post_ffn[i][None],
                           w_ple[i].astype(BF16), w_ple_gate[i].astype(BF16), g_ple[i][None])
    return xc.reshape(b, t, d)
```
